```python
import jax, jax.numpy as jnp
from jax import lax
import numpy as np


D_MODEL = 1024
BATCH = 8
SEQ = 4096
DEPTH = 4
DEC_BATCH = 4
DEC_SEQ = 4096
PAST_LEN = 128

GRID_W = 64
N_EVEN = (DEPTH + 1) // 2
N_ODD = DEPTH // 2
EPS = 1e-6

POOL_WIDTH = D_MODEL // 2
POOL_WINDOWS = (2, 4, 8, 16)
POOL_GROUPS = len(POOL_WINDOWS)
POOL_GC = POOL_WIDTH // POOL_GROUPS

HEAD_DIM = 64
N_Q_HEADS = (D_MODEL // 2) // HEAD_DIM
N_KV_HEADS = 2
Q_PER_KV = N_Q_HEADS // N_KV_HEADS
ATTN_WIDTH = N_Q_HEADS * HEAD_DIM
KV_WIDTH = N_KV_HEADS * HEAD_DIM
ROPE_HALF = HEAD_DIM // 2
ROPE_FREQ = ROPE_HALF // 2
ROPE_THETA = 10000.0
Q_BLOCK = 128

EVEN_SIZES = (POOL_WIDTH, POOL_WIDTH, ATTN_WIDTH, KV_WIDTH, KV_WIDTH, ATTN_WIDTH)
EVEN_IN = sum(EVEN_SIZES)
EVEN_SPLITS = tuple(int(s) for s in np.cumsum(EVEN_SIZES)[:-1])
EVEN_MIX = POOL_WIDTH + ATTN_WIDTH

SGU_WIDTH = D_MODEL
SGU_GROUPS = 8
SGU_GC = SGU_WIDTH // SGU_GROUPS
CHUNK = 128
ODD_IN = 3 * SGU_WIDTH

kernel_name = 'hybrid_pool_axialgqa_gmlp_encoder'


def rms_norm(x, g):
    xf = x.astype(jnp.float32)
    y = xf * lax.rsqrt(jnp.mean(xf * xf, axis=-1, keepdims=True) + EPS)
    return (y * g.astype(jnp.float32)).astype(x.dtype)


def rope_tables(L):
    rows_n = L // GRID_W
    row = jnp.repeat(jnp.arange(rows_n), GRID_W).astype(jnp.float32)
    col = jnp.tile(jnp.arange(GRID_W), rows_n).astype(jnp.float32)
    inv = 1.0 / (ROPE_THETA ** (jnp.arange(ROPE_FREQ, dtype=jnp.float32) / ROPE_FREQ))
    ang_r = row[:, None] * inv[None, :]
    ang_c = col[:, None] * inv[None, :]
    return (jnp.cos(ang_r), jnp.sin(ang_r), jnp.cos(ang_c), jnp.sin(ang_c))


def rope_1d(x, cos, sin):
    x1, x2 = x[..., :ROPE_FREQ], x[..., ROPE_FREQ:]
    return jnp.concatenate([x1 * cos - x2 * sin, x2 * cos + x1 * sin], axis=-1)


def rope_2d(x, tables):
    cos_r, sin_r, cos_c, sin_c = tables
    xf = x.astype(jnp.float32)
    xr = rope_1d(xf[..., :ROPE_HALF], cos_r[:, None, :], sin_r[:, None, :])
    xc = rope_1d(xf[..., ROPE_HALF:], cos_c[:, None, :], sin_c[:, None, :])
    return jnp.concatenate([xr, xc], axis=-1).astype(x.dtype)


def pool_mixer(u, pool_w, pool_scale):
    B, L, _ = u.shape
    uf = u.astype(jnp.float32)
    cs = jnp.concatenate([jnp.zeros((B, 1, POOL_WIDTH), jnp.float32), jnp.cumsum(uf, axis=1)], axis=1)
    t = jnp.arange(L)
    means = []
    for g, w in enumerate(POOL_WINDOWS):
        lo = jnp.clip(t - w // 2, 0, L)
        hi = jnp.clip(t + w // 2, 0, L)
        cnt = (hi - lo).astype(jnp.float32)
        csg = cs[..., g * POOL_GC:(g + 1) * POOL_GC]
        means.append((jnp.take(csg, hi, axis=1) - jnp.take(csg, lo, axis=1)) / cnt[None, :, None])
    pooled = jnp.stack(means, axis=2)
    diff = (pooled - uf.reshape(B, L, POOL_GROUPS, POOL_GC)).astype(u.dtype)
    mixed = jnp.einsum('blgc,gcd->blgd', diff, pool_w).reshape(B, L, POOL_WIDTH)
    return mixed * pool_scale


def axial_gqa(q, k, v, q_g, k_g, tables):
    B, L, _ = q.shape
    nb = L // Q_BLOCK
    q = rope_2d(rms_norm(q.reshape(B, L, N_Q_HEADS, HEAD_DIM), q_g), tables)
    k = rope_2d(rms_norm(k.reshape(B, L, N_KV_HEADS, HEAD_DIM), k_g), tables)
    v = v.reshape(B, L, N_KV_HEADS, HEAD_DIM)
    qb = q.reshape(B, nb, Q_BLOCK, N_KV_HEADS, Q_PER_KV, HEAD_DIM).transpose(1, 0, 2, 3, 4, 5)
    scale = HEAD_DIM ** -0.5

    def block(q_blk):
        s = jnp.einsum('bqkgd,bskd->bkgqs', q_blk, k, preferred_element_type=jnp.float32) * scale
        p = jax.nn.softmax(s, axis=-1).astype(v.dtype)
        return jnp.einsum('bkgqs,bskd->bqkgd', p, v)

    o = lax.map(block, qb)
    return o.transpose(1, 0, 2, 3, 4, 5).reshape(B, L, ATTN_WIDTH)


def even_layer(x, norm_g, w_in, pool_w, pool_scale, q_g, k_g, w_out, tables):
    h = rms_norm(x, norm_g)
    p = h @ w_in
    pu, pz, q, k, v, az = jnp.split(p, EVEN_SPLITS, axis=-1)
    a_out = pool_mixer(pu, pool_w, pool_scale) * jax.nn.silu(pz)
    b_out = axial_gqa(q, k, v, q_g, k_g, tables) * jax.nn.silu(az)
    return x + jnp.concatenate([a_out, b_out], axis=-1) @ w_out


def odd_layer(x, norm_g, w_in, sgu_g, w_s, b_s, w_out):
    B, L, _ = x.shape
    h = rms_norm(x, norm_g)
    u, vv, z = jnp.split(h @ w_in, 3, axis=-1)
    u = jax.nn.gelu(u, approximate=False)
    vv = rms_norm(jax.nn.gelu(vv, approximate=False), sgu_g)
    vc = vv.reshape(B, L // CHUNK, CHUNK, SGU_GROUPS, SGU_GC)
    sv = jnp.einsum('hpq,bnqhc->bnphc', w_s, vc) + b_s.T[None, None, :, :, None]
    y = u * sv.reshape(B, L, SGU_WIDTH) * jax.nn.silu(z)
    return x + y @ w_out


def trunk(x, norm_e, w_in_e, pool_w, pool_scale, q_norm, k_norm, w_out_e,
          norm_o, w_in_o, sgu_norm, w_s, b_s, w_out_o):
    tables = rope_tables(x.shape[1])
    for i in range(DEPTH):
        j = i // 2
        if i % 2 == 0:
            x = even_layer(x, norm_e[j], w_in_e[j], pool_w[j], pool_scale[j], q_norm[j], k_norm[j], w_out_e[j], tables)
        else:
            x = odd_layer(x, norm_o[j], w_in_o[j], sgu_norm[j], w_s[j], b_s[j], w_out_o[j])
    return x


def setup_inputs(seed: int = 0) -> dict:
    key = jax.random.key(seed)
    ks = jax.random.split(key, 16)
    f32 = jnp.float32
    nrm = lambda k, shape, s: jax.random.normal(k, shape, f32) * s
    return {
        'x_prompt': jax.random.normal(ks[0], (BATCH, SEQ, D_MODEL), f32),
        'x_sample': jax.random.normal(ks[1], (DEC_BATCH, DEC_SEQ, D_MODEL), f32),
        'norm_e': 1.0 + nrm(ks[2], (N_EVEN, D_MODEL), 0.02),
        'w_in_e': nrm(ks[3], (N_EVEN, D_MODEL, EVEN_IN), D_MODEL ** -0.5),
        'pool_w': nrm(ks[4], (N_EVEN, POOL_GROUPS, POOL_GC, POOL_GC), POOL_GC ** -0.5),
        'pool_scale': 1.0 + nrm(ks[5], (N_EVEN, POOL_WIDTH), 0.02),
        'q_norm': 1.0 + nrm(ks[6], (N_EVEN, HEAD_DIM), 0.02),
        'k_norm': 1.0 + nrm(ks[7], (N_EVEN, HEAD_DIM), 0.02),
        'w_out_e': nrm(ks[8], (N_EVEN, EVEN_MIX, D_MODEL), EVEN_MIX ** -0.5),
        'norm_o': 1.0 + nrm(ks[9], (N_ODD, D_MODEL), 0.02),
        'w_in_o': nrm(ks[10], (N_ODD, D_MODEL, ODD_IN), D_MODEL ** -0.5),
        'sgu_norm': 1.0 + nrm(ks[11], (N_ODD, SGU_WIDTH), 0.02),
        'w_s': nrm(ks[12], (N_ODD, SGU_GROUPS, CHUNK, CHUNK), CHUNK ** -0.5),
        'b_s': 1.0 + nrm(ks[13], (N_ODD, SGU_GROUPS, CHUNK), 0.02),
        'w_out_o': nrm(ks[14], (N_ODD, SGU_WIDTH, D_MODEL), SGU_WIDTH ** -0.5),
    }


def reference(x_prompt, x_sample, norm_e, w_in_e, pool_w, pool_scale, q_norm, k_norm, w_out_e,
              norm_o, w_in_o, sgu_norm, w_s, b_s, w_out_o):
    y_prompt = trunk(x_prompt, norm_e, w_in_e, pool_w, pool_scale, q_norm, k_norm, w_out_e,
                     norm_o, w_in_o, sgu_norm, w_s, b_s, w_out_o)
    y_sample = trunk(x_sample, norm_e, w_in_e, pool_w, pool_scale, q_norm, k_norm, w_out_e,
                     norm_o, w_in_o, sgu_norm, w_s, b_s, w_out_o)
    return (y_prompt, y_sample)
```

```python
import functools
import math

import jax
import jax.numpy as jnp
import numpy as np
from jax import lax
from jax.experimental import pallas as pl
from jax.experimental.pallas import tpu as pltpu

F32 = jnp.float32
BF16 = jnp.bfloat16

D_MODEL = 1024
DEPTH = 4
GRID_W = 64
EPS = 1e-6

POOL_WIDTH = 512
POOL_WINDOWS = (2, 4, 8, 16)
POOL_GC = 128

HEAD_DIM = 64
N_Q_HEADS = 8
N_KV_HEADS = 2
Q_PER_KV = 4
ATTN_WIDTH = 512
KV_WIDTH = 128
ROPE_FREQ = 16
ROPE_THETA = 10000.0

OFF_PU, OFF_PZ, OFF_Q, OFF_K, OFF_V, OFF_AZ, EVEN_IN = 0, 512, 1024, 1536, 1664, 1792, 2304

SGU_WIDTH = 1024
SGU_GROUPS = 8
SGU_GC = 128
CHUNK = 128

LANES = 128
HALO = 8

ROW_TILE = 512
Q_TILE = 128
KEY_TILE = 512
ONES_ROWS = 16

VMEM_LIMIT = 56 * 1024 * 1024


def _silu(x):
    return x * (1.0 / (1.0 + jnp.exp(-x)))


def _gelu(x):
    return 0.5 * x * (1.0 + lax.erf(x * np.float32(math.sqrt(0.5))))


def _rms_rows(x, g):
    ms = jnp.mean(x * x, axis=-1, keepdims=True)
    return x * lax.rsqrt(ms + EPS) * g


def _even_in_kernel(x_ref, ng_ref, w_ref, qg_ref, kg_ref, cos_ref, sin_ref, gm_ref,
                    pu_ref, pz_ref, az_ref, qt_ref, k_ref, vt_ref):
    h = _rms_rows(x_ref[0], ng_ref[...]).astype(BF16)

    def proj(lo, hi):
        return jnp.dot(h, w_ref[:, lo:hi], preferred_element_type=F32)

    pu_ref[0] = proj(OFF_PU, OFF_PZ)
    pz_ref[0] = proj(OFF_PZ, OFF_Q)
    az_ref[0] = proj(OFF_AZ, EVEN_IN)

    cos = cos_ref[...]
    sin = sin_ref[...]
    gm = gm_ref[...]
    lane = lax.broadcasted_iota(jnp.int32, cos.shape, 1)
    first_half = (lane % (2 * ROPE_FREQ)) < ROPE_FREQ

    def norm_rope(t, g):
        sq = t * t
        hi = sq.astype(BF16)
        lo = (sq - hi.astype(F32)).astype(BF16)
        ms = (jnp.dot(hi, gm, preferred_element_type=F32)
              + jnp.dot(lo, gm, preferred_element_type=F32))
        y = t * lax.rsqrt(ms + EPS) * g
        partner = jnp.where(first_half, pltpu.roll(y, LANES - ROPE_FREQ, 1),
                            pltpu.roll(y, ROPE_FREQ, 1))
        return y * cos + partner * sin

    q = proj(OFF_Q, OFF_K)
    for j in range(ATTN_WIDTH // LANES):
        sl = slice(j * LANES, (j + 1) * LANES)
        qj = norm_rope(q[:, sl], qg_ref[:, sl]) * np.float32(HEAD_DIM ** -0.5)
        qt_ref[0, sl, :] = qj.T.astype(BF16)
    k_ref[0] = norm_rope(proj(OFF_K, OFF_V), kg_ref[...]).astype(BF16)
    vt_ref[0] = proj(OFF_V, OFF_AZ).T.astype(BF16)


def _even_in(x, norm_g, w_in, q_g, k_g, cos_t, sin_t, gmat):
    B, L, _ = x.shape
    tm = ROW_TILE
    nt = L // tm
    full = lambda shape: pl.BlockSpec(shape, lambda b, i: (0,) * len(shape))
    tok = lambda w: pl.BlockSpec((1, tm, w), lambda b, i: (b, i, 0))
    tokt = lambda w: pl.BlockSpec((1, w, tm), lambda b, i: (b, 0, i))
    return pl.pallas_call(
        _even_in_kernel,
        grid=(B, nt),
        in_specs=[
            tok(D_MODEL),
            full((1, D_MODEL)),
            full((D_MODEL, EVEN_IN)),
            full((1, ATTN_WIDTH)),
            full((1, KV_WIDTH)),
            pl.BlockSpec((tm, LANES), lambda b, i: (i, 0)),
            pl.BlockSpec((tm, LANES), lambda b, i: (i, 0)),
            full((LANES, LANES)),
        ],
        out_specs=[tok(POOL_WIDTH), tok(POOL_WIDTH), tok(ATTN_WIDTH),
                   tokt(ATTN_WIDTH), tok(KV_WIDTH), tokt(KV_WIDTH)],
        out_shape=[
            jax.ShapeDtypeStruct((B, L, POOL_WIDTH), F32),
            jax.ShapeDtypeStruct((B, L, POOL_WIDTH), F32),
            jax.ShapeDtypeStruct((B, L, ATTN_WIDTH), F32),
            jax.ShapeDtypeStruct((B, ATTN_WIDTH, L), BF16),
            jax.ShapeDtypeStruct((B, L, KV_WIDTH), BF16),
            jax.ShapeDtypeStruct((B, KV_WIDTH, L), BF16),
        ],
        compiler_params=pltpu.CompilerParams(
            dimension_semantics=("arbitrary", "arbitrary"), vmem_limit_bytes=VMEM_LIMIT),
        name="even_in",
    )(x, norm_g, w_in, q_g, k_g, cos_t, sin_t, gmat)


def _attn_kernel(qt_ref, k_ref, vt_ref, o_ref, s_ref):
    kvh = pl.program_id(1)
    L = k_ref.shape[1]
    nc = L // KEY_TILE
    qt = qt_ref[0]
    qcat = jnp.concatenate(
        [qt[g * HEAD_DIM:(g + 1) * HEAD_DIM, :] for g in range(Q_PER_KV)], axis=1)
    zero = jnp.zeros_like(qcat)
    qx = jnp.concatenate([jnp.where(kvh == 0, qcat, zero),
                          jnp.where(kvh == 1, qcat, zero)], axis=0)
    m = None
    for c in range(nc):
        s_c = jnp.dot(k_ref[0, c * KEY_TILE:(c + 1) * KEY_TILE, :], qx,
                      preferred_element_type=F32)
        s_ref[c] = s_c
        mc = jnp.max(s_c, axis=0, keepdims=True)
        m = mc if m is None else jnp.maximum(m, mc)
    vx = jnp.concatenate([vt_ref[0], jnp.ones((ONES_ROWS, L), BF16)], axis=0)
    acc = None
    for c in range(nc):
        p = jnp.exp(s_ref[c] - m).astype(BF16)
        d = jnp.dot(vx[:, c * KEY_TILE:(c + 1) * KEY_TILE], p, preferred_element_type=F32)
        acc = d if acc is None else acc + d
    out = acc[0:HEAD_DIM] * (1.0 / acc[HEAD_DIM:HEAD_DIM + 1])
    tq = qt.shape[1]
    for pair in range(Q_PER_KV // 2):
        two = jnp.concatenate([out[:, (2 * pair) * tq:(2 * pair + 1) * tq],
                               out[:, (2 * pair + 1) * tq:(2 * pair + 2) * tq]], axis=0)
        o_ref[0, :, pair * LANES:(pair + 1) * LANES] = two.T


def _attention(qt, k, vt):
    B, _, L = qt.shape
    tq = Q_TILE
    gw = Q_PER_KV * HEAD_DIM
    return pl.pallas_call(
        _attn_kernel,
        grid=(B, N_KV_HEADS, L // tq),
        in_specs=[
            pl.BlockSpec((1, gw, tq), lambda b, h, i: (b, h, i)),
            pl.BlockSpec((1, L, KV_WIDTH), lambda b, h, i: (b, 0, 0)),
            pl.BlockSpec((1, HEAD_DIM, L), lambda b, h, i: (b, h, 0)),
        ],
        out_specs=pl.BlockSpec((1, tq, gw), lambda b, h, i: (b, i, h)),
        out_shape=jax.ShapeDtypeStruct((B, L, ATTN_WIDTH), F32),
        scratch_shapes=[pltpu.VMEM((L // KEY_TILE, KEY_TILE, Q_PER_KV * tq), F32)],
        compiler_params=pltpu.CompilerParams(
            dimension_semantics=("arbitrary", "arbitrary", "arbitrary"),
            vmem_limit_bytes=VMEM_LIMIT),
        name="attention",
    )(qt, k, vt)


def _even_out_kernel(pu_ref, pp_ref, pn_ref, pz_ref, o_ref, az_ref, x_ref, pw_ref, ps_ref, wo_ref,
                     out_ref, ext_ref):
    it = pl.program_id(1)
    nt = pl.num_programs(1)
    tm = pu_ref.shape[1]
    L = tm * nt
    u = pu_ref[0]
    ext_ref[0:HALO] = jnp.where(it > 0, pp_ref[0], 0.0)
    ext_ref[HALO:HALO + tm] = u
    ext_ref[HALO + tm:2 * HALO + tm] = jnp.where(it < nt - 1, pn_ref[0], 0.0)
    t = it * tm + lax.broadcasted_iota(jnp.int32, (tm, 1), 0)
    parts = []
    for g, w in enumerate(POOL_WINDOWS):
        hw = w // 2
        sl = slice(g * POOL_GC, (g + 1) * POOL_GC)
        acc = None
        for j in range(-hw, hw):
            piece = ext_ref[HALO + j:HALO + j + tm, sl]
            acc = piece if acc is None else acc + piece
        cnt = (jnp.minimum(t + hw, L) - jnp.maximum(t - hw, 0)).astype(F32)
        diff = (acc / cnt - u[:, sl]).astype(BF16)
        parts.append(jnp.dot(diff, pw_ref[g], preferred_element_type=F32))
    mixed = jnp.concatenate(parts, axis=1) * ps_ref[...]
    a_out = mixed * _silu(pz_ref[0])
    b_out = o_ref[0] * _silu(az_ref[0])
    cat = jnp.concatenate([a_out, b_out], axis=1).astype(BF16)
    out_ref[0] = x_ref[0] + jnp.dot(cat, wo_ref[...], preferred_element_type=F32)


def _even_out(pu, pz, o, az, x, pool_w, pool_scale, w_out):
    B, L, _ = x.shape
    tm = ROW_TILE
    nt = L // tm
    hb = tm // HALO
    full = lambda shape: pl.BlockSpec(shape, lambda b, i: (0,) * len(shape))
    tok = lambda w: pl.BlockSpec((1, tm, w), lambda b, i: (b, i, 0))
    return pl.pallas_call(
        _even_out_kernel,
        grid=(B, nt),
        in_specs=[
            tok(POOL_WIDTH),
            pl.BlockSpec((1, HALO, POOL_WIDTH), lambda b, i: (b, jnp.maximum(i * hb - 1, 0), 0)),
            pl.BlockSpec((1, HALO, POOL_WIDTH),
                         lambda b, i: (b, jnp.minimum((i + 1) * hb, L // HALO - 1), 0)),
            tok(POOL_WIDTH), tok(ATTN_WIDTH), tok(ATTN_WIDTH), tok(D_MODEL),
            full((len(POOL_WINDOWS), POOL_GC, POOL_GC)),
            full((1, POOL_WIDTH)),
            full((D_MODEL, D_MODEL)),
        ],
        out_specs=tok(D_MODEL),
        out_shape=jax.ShapeDtypeStruct((B, L, D_MODEL), F32),
        scratch_shapes=[pltpu.VMEM((tm + 2 * HALO, POOL_WIDTH), F32)],
        compiler_params=pltpu.CompilerParams(
            dimension_semantics=("arbitrary", "arbitrary"), vmem_limit_bytes=VMEM_LIMIT),
        name="even_out",
    )(pu, pu, pu, pz, o, az, x, pool_w, pool_scale, w_out)


def _odd_kernel(x_ref, ng_ref, wi_ref, sg_ref, ws_ref, bs_ref, wo_ref, out_ref, sv_ref):
    x = x_ref[0]
    tm = x.shape[0]
    h = _rms_rows(x, ng_ref[...]).astype(BF16)

    def proj(j):
        return jnp.dot(h, wi_ref[:, j * SGU_WIDTH:(j + 1) * SGU_WIDTH], preferred_element_type=F32)

    vv = _rms_rows(_gelu(proj(1)), sg_ref[...]).astype(BF16)
    nch = tm // CHUNK
    for g in range(SGU_GROUPS):
        gs = slice(g * SGU_GC, (g + 1) * SGU_GC)
        vcat = jnp.concatenate([vv[n * CHUNK:(n + 1) * CHUNK, gs] for n in range(nch)], axis=1)
        res = jnp.dot(ws_ref[g], vcat, preferred_element_type=F32)
        for n in range(nch):
            sv_ref[n * CHUNK:(n + 1) * CHUNK, gs] = res[:, n * SGU_GC:(n + 1) * SGU_GC] + bs_ref[g]
    y = (_gelu(proj(0)) * sv_ref[...] * _silu(proj(2))).astype(BF16)
    out_ref[0] = x + jnp.dot(y, wo_ref[...], preferred_element_type=F32)


def _odd_layer(x, norm_g, w_in, sgu_g, w_s, b_s, w_out):
    B, L, _ = x.shape
    tm = ROW_TILE
    full = lambda shape: pl.BlockSpec(shape, lambda b, i: (0,) * len(shape))
    tok = pl.BlockSpec((1, tm, D_MODEL), lambda b, i: (b, i, 0))
    return pl.pallas_call(
        _odd_kernel,
        grid=(B, L // tm),
        in_specs=[
            tok,
            full((1, D_MODEL)),
            full((D_MODEL, 3 * SGU_WIDTH)),
            full((1, SGU_WIDTH)),
            full((SGU_GROUPS, CHUNK, CHUNK)),
            full((SGU_GROUPS, CHUNK, SGU_GC)),
            full((SGU_WIDTH, D_MODEL)),
        ],
        out_specs=tok,
        out_shape=jax.ShapeDtypeStruct((B, L, D_MODEL), F32),
        scratch_shapes=[pltpu.VMEM((tm, SGU_WIDTH), F32)],
        compiler_params=pltpu.CompilerParams(
            dimension_semantics=("arbitrary", "arbitrary"), vmem_limit_bytes=VMEM_LIMIT),
        name="odd_layer",
    )(x, norm_g, w_in, sgu_g, w_s, b_s, w_out)


def _rope_tables(L):
    rows_n = L // GRID_W
    row = jnp.repeat(jnp.arange(rows_n), GRID_W).astype(F32)
    col = jnp.tile(jnp.arange(GRID_W), rows_n).astype(F32)
    inv = 1.0 / (ROPE_THETA ** (jnp.arange(ROPE_FREQ, dtype=F32) / ROPE_FREQ))
    ang_r = row[:, None] * inv[None, :]
    ang_c = col[:, None] * inv[None, :]
    cos_h = jnp.concatenate([jnp.cos(ang_r), jnp.cos(ang_r), jnp.cos(ang_c), jnp.cos(ang_c)], axis=-1)
    sin_h = jnp.concatenate([-jnp.sin(ang_r), jnp.sin(ang_r), -jnp.sin(ang_c), jnp.sin(ang_c)], axis=-1)
    reps = LANES // HEAD_DIM
    return jnp.tile(cos_h, (1, reps)), jnp.tile(sin_h, (1, reps))


def _head_mean_matrix():
    idx = np.arange(LANES) // HEAD_DIM
    return jnp.asarray((idx[:, None] == idx[None, :]).astype(np.float32) / HEAD_DIM, dtype=BF16)


def _trunk(x, params, cos_t, sin_t, gmat):
    (norm_e, w_in_e, pool_w, pool_scale, q_norm, k_norm, w_out_e,
     norm_o, w_in_o, sgu_norm, w_s, b_s, w_out_o) = params
    for i in range(DEPTH):
        j = i // 2
        if i % 2 == 0:
            pu, pz, az, qt, k, vt = _even_in(
                x, norm_e[j][None, :], w_in_e[j],
                jnp.tile(q_norm[j], N_Q_HEADS)[None, :], jnp.tile(k_norm[j], N_KV_HEADS)[None, :],
                cos_t, sin_t, gmat)
            o = _attention(qt, k, vt)
            x = _even_out(pu, pz, o, az, x, pool_w[j], pool_scale[j][None, :], w_out_e[j])
        else:
            bs_b = jnp.broadcast_to(b_s[j][:, :, None], (SGU_GROUPS, CHUNK, SGU_GC))
            x = _odd_layer(x, norm_o[j][None, :], w_in_o[j], sgu_norm[j][None, :], w_s[j], bs_b,
                           w_out_o[j])
    return x


def kernel(x_prompt, x_sample, norm_e, w_in_e, pool_w, pool_scale, q_norm, k_norm, w_out_e,
           norm_o, w_in_o, sgu_norm, w_s, b_s, w_out_o):
    params = (norm_e, w_in_e.astype(BF16), pool_w.astype(BF16), pool_scale, q_norm, k_norm,
              w_out_e.astype(BF16), norm_o, w_in_o.astype(BF16), sgu_norm, w_s.astype(BF16), b_s,
              w_out_o.astype(BF16))
    cos_t, sin_t = _rope_tables(x_prompt.shape[1])
    gmat = _head_mean_matrix()
    y_prompt = _trunk(x_prompt, params, cos_t, sin_t, gmat)
    y_sample = _trunk(x_sample, params, cos_t, sin_t, gmat)
    return (y_prompt, y_sample)
```

```python
import functools
import math

import jax
import jax.numpy as jnp
import numpy as np
from jax import lax
from jax.experimental import pallas as pl
from jax.experimental.pallas import tpu as pltpu

F32 = jnp.float32
BF16 = jnp.bfloat16

D_MODEL = 1024
DEPTH = 4
GRID_W = 64
EPS = 1e-6

POOL_WIDTH = 512
POOL_WINDOWS = (2, 4, 8, 16)
POOL_GC = 128

HEAD_DIM = 64
N_Q_HEADS = 8
N_KV_HEADS = 2
Q_PER_KV = 4
ATTN_WIDTH = 512
KV_WIDTH = 128
ROPE_FREQ = 16
ROPE_THETA = 10000.0

OFF_PU, OFF_PZ, OFF_Q, OFF_K, OFF_V, OFF_AZ, EVEN_IN = 0, 512, 1024, 1536, 1664, 1792, 2304

SGU_WIDTH = 1024
SGU_GROUPS = 8
SGU_GC = 128
CHUNK = 128

LANES = 128
HALO = 8

ROW_TILE = 512
Q_TILE = 128
KEY_TILE = 512
ONES_ROWS = 16
Q_PRESCALE = HEAD_DIM ** -0.5 * math.log2(math.e)

VMEM_LIMIT = 56 * 1024 * 1024


def _silu(x):
    return x * (1.0 / (1.0 + jnp.exp(-x)))


def _gelu(x):
    return 0.5 * x * (1.0 + lax.erf(x * np.float32(math.sqrt(0.5))))


def _rms_rows(x, g):
    ms = jnp.mean(x * x, axis=-1, keepdims=True)
    return x * lax.rsqrt(ms + EPS) * g


def _even_in_kernel(x_ref, ng_ref, w_ref, qg_ref, kg_ref, cos_ref, sin_ref, gm_ref, cost_ref, sint_ref,
                    pu_ref, pz_ref, az_ref, qt_ref, k_ref, vx_ref):
    h = _rms_rows(x_ref[0], ng_ref[...]).astype(BF16)

    def proj(lo, hi):
        return jnp.dot(h, w_ref[:, lo:hi], preferred_element_type=F32)

    qt = proj(OFF_Q, OFF_K).T
    cos_r, cos_c = cost_ref[0:ROPE_FREQ], cost_ref[ROPE_FREQ:2 * ROPE_FREQ]
    sin_r, sin_c = sint_ref[0:ROPE_FREQ], sint_ref[ROPE_FREQ:2 * ROPE_FREQ]
    for hd in range(N_Q_HEADS):
        rows = slice(hd * HEAD_DIM, (hd + 1) * HEAD_DIM)
        t = qt[rows]
        ms = jnp.sum(t * t, axis=0, keepdims=True) * np.float32(1.0 / HEAD_DIM)
        y = t * (lax.rsqrt(ms + EPS) * np.float32(Q_PRESCALE)) * qg_ref[rows]
        x1, x2, x3, x4 = (y[j * ROPE_FREQ:(j + 1) * ROPE_FREQ] for j in range(4))
        out = jnp.concatenate([x1 * cos_r - x2 * sin_r, x2 * cos_r + x1 * sin_r,
                               x3 * cos_c - x4 * sin_c, x4 * cos_c + x3 * sin_c], axis=0)
        qt_ref[0, rows, :] = out.astype(BF16)

    cos = cos_ref[...]
    sin = sin_ref[...]
    gm = gm_ref[...]
    lane = lax.broadcasted_iota(jnp.int32, cos.shape, 1)
    first_half = (lane % (2 * ROPE_FREQ)) < ROPE_FREQ
    t = proj(OFF_K, OFF_V)
    sq = t * t
    hi = sq.astype(BF16)
    lo = (sq - hi.astype(F32)).astype(BF16)
    ms = (jnp.dot(hi, gm, preferred_element_type=F32) + jnp.dot(lo, gm, preferred_element_type=F32))
    y = t * lax.rsqrt(ms + EPS) * kg_ref[...]
    partner = jnp.where(first_half, pltpu.roll(y, LANES - ROPE_FREQ, 1), pltpu.roll(y, ROPE_FREQ, 1))
    k_ref[0] = (y * cos + partner * sin).astype(BF16)

    vt = proj(OFF_V, OFF_AZ).T
    ones = jnp.ones((ONES_ROWS, vt.shape[1]), BF16)
    for kv in range(N_KV_HEADS):
        vx_ref[0, kv, 0:HEAD_DIM] = vt[kv * HEAD_DIM:(kv + 1) * HEAD_DIM].astype(BF16)
        vx_ref[0, kv, HEAD_DIM:HEAD_DIM + ONES_ROWS] = ones

    pu_ref[0] = proj(OFF_PU, OFF_PZ)
    pz_ref[0] = proj(OFF_PZ, OFF_Q)
    az_ref[0] = proj(OFF_AZ, EVEN_IN)


def _even_in(x, norm_g, w_in, q_g, k_g, cos_t, sin_t, gmat, cos_tt, sin_tt):
    B, L, _ = x.shape
    tm = ROW_TILE
    nt = L // tm
    full = lambda shape: pl.BlockSpec(shape, lambda b, i: (0,) * len(shape))
    tok = lambda w: pl.BlockSpec((1, tm, w), lambda b, i: (b, i, 0))
    vrows = HEAD_DIM + ONES_ROWS
    return pl.pallas_call(
        _even_in_kernel,
        grid=(B, nt),
        in_specs=[
            tok(D_MODEL),
            full((1, D_MODEL)),
            full((D_MODEL, EVEN_IN)),
            full((ATTN_WIDTH, tm)),
            full((1, KV_WIDTH)),
            pl.BlockSpec((tm, LANES), lambda b, i: (i, 0)),
            pl.BlockSpec((tm, LANES), lambda b, i: (i, 0)),
            full((LANES, LANES)),
            pl.BlockSpec((2 * ROPE_FREQ, tm), lambda b, i: (0, i)),
            pl.BlockSpec((2 * ROPE_FREQ, tm), lambda b, i: (0, i)),
        ],
        out_specs=[tok(POOL_WIDTH), tok(POOL_WIDTH), tok(ATTN_WIDTH),
                   pl.BlockSpec((1, ATTN_WIDTH, tm), lambda b, i: (b, 0, i)),
                   tok(KV_WIDTH),
                   pl.BlockSpec((1, N_KV_HEADS, vrows, tm), lambda b, i: (b, 0, 0, i))],
        out_shape=[
            jax.ShapeDtypeStruct((B, L, POOL_WIDTH), F32),
            jax.ShapeDtypeStruct((B, L, POOL_WIDTH), F32),
            jax.ShapeDtypeStruct((B, L, ATTN_WIDTH), F32),
            jax.ShapeDtypeStruct((B, ATTN_WIDTH, L), BF16),
            jax.ShapeDtypeStruct((B, L, KV_WIDTH), BF16),
            jax.ShapeDtypeStruct((B, N_KV_HEADS, vrows, L), BF16),
        ],
        compiler_params=pltpu.CompilerParams(
            dimension_semantics=("arbitrary", "arbitrary"), vmem_limit_bytes=VMEM_LIMIT),
        name="even_in",
    )(x, norm_g, w_in, q_g, k_g, cos_t, sin_t, gmat, cos_tt, sin_tt)


def _attn_kernel(qt_ref, k_ref, vx_ref, o_ref, s_ref, m_ref):
    i = pl.program_id(0)
    n_tiles = pl.num_programs(0) - 1
    L = k_ref.shape[1]
    tq = qt_ref.shape[2]
    nq = L // tq
    nc = L // KEY_TILE
    R = Q_PER_KV * tq

    @pl.when(i == 0)
    def _():
        s_ref[...] = jnp.zeros(s_ref.shape, F32)
        m_ref[...] = jnp.zeros(m_ref.shape, F32)

    kvh = (jnp.minimum(i, n_tiles - 1) // nq) % N_KV_HEADS
    qt = qt_ref[0]
    qcat = jnp.concatenate(
        [qt[g * HEAD_DIM:(g + 1) * HEAD_DIM, :] for g in range(Q_PER_KV)], axis=1)
    zero = jnp.zeros_like(qcat)
    qx = jnp.concatenate([jnp.where(kvh == 0, qcat, zero),
                          jnp.where(kvh == 1, qcat, zero)], axis=0)
    m_prev = jnp.max(m_ref[...], axis=0, keepdims=True)
    m_new = None
    acc = None
    for c in range(nc):
        keys = slice(c * KEY_TILE, (c + 1) * KEY_TILE)
        p = jnp.exp2(s_ref[c] - m_prev).astype(BF16)
        d = jnp.dot(vx_ref[0, 0, :, keys], p, preferred_element_type=F32)
        acc = d if acc is None else acc + d
        s_c = jnp.dot(k_ref[0, keys, :], qx, preferred_element_type=F32)
        s_ref[c] = s_c
        mc = jnp.max(s_c.reshape(KEY_TILE // HALO, HALO, R), axis=0)
        m_new = mc if m_new is None else jnp.maximum(m_new, mc)
    m_ref[...] = m_new
    out = acc[0:HEAD_DIM] * (1.0 / acc[HEAD_DIM:HEAD_DIM + 1])
    for pair in range(Q_PER_KV // 2):
        two = jnp.concatenate([out[:, (2 * pair) * tq:(2 * pair + 1) * tq],
                               out[:, (2 * pair + 1) * tq:(2 * pair + 2) * tq]], axis=0)
        o_ref[0, :, pair * LANES:(pair + 1) * LANES] = two.T


def _attention(qt, k, vx):
    B, _, L = qt.shape
    tq = Q_TILE
    nq = L // tq
    gw = Q_PER_KV * HEAD_DIM
    n_tiles = B * N_KV_HEADS * nq

    def tile(t):
        return t // (N_KV_HEADS * nq), (t // nq) % N_KV_HEADS, t % nq

    def cur(i):
        return tile(jnp.minimum(i, n_tiles - 1))

    def prev(i):
        return tile(jnp.maximum(i - 1, 0))

    def q_map(i):
        b, h, iq = cur(i)
        return (b, h, iq)

    def k_map(i):
        return (cur(i)[0], 0, 0)

    def v_map(i):
        b, h, _ = prev(i)
        return (b, h, 0, 0)

    def o_map(i):
        b, h, iq = prev(i)
        return (b, iq, h)

    return pl.pallas_call(
        _attn_kernel,
        grid=(n_tiles + 1,),
        in_specs=[
            pl.BlockSpec((1, gw, tq), q_map),
            pl.BlockSpec((1, L, KV_WIDTH), k_map),
            pl.BlockSpec((1, 1, HEAD_DIM + ONES_ROWS, L), v_map),
        ],
        out_specs=pl.BlockSpec((1, tq, gw), o_map),
        out_shape=jax.ShapeDtypeStruct((B, L, ATTN_WIDTH), F32),
        scratch_shapes=[pltpu.VMEM((L // KEY_TILE, KEY_TILE, Q_PER_KV * tq), F32),
                        pltpu.VMEM((HALO, Q_PER_KV * tq), F32)],
        compiler_params=pltpu.CompilerParams(
            dimension_semantics=("arbitrary",), vmem_limit_bytes=VMEM_LIMIT),
        name="attention",
    )(qt, k, vx)


def _even_out_kernel(pu_ref, pp_ref, pn_ref, pz_ref, o_ref, az_ref, x_ref, pw_ref, ps_ref, wo_ref,
                     out_ref, ext_ref):
    it = pl.program_id(1)
    nt = pl.num_programs(1)
    tm = pu_ref.shape[1]
    L = tm * nt
    u = pu_ref[0]
    ext_ref[0:HALO] = jnp.where(it > 0, pp_ref[0], 0.0)
    ext_ref[HALO:HALO + tm] = u
    ext_ref[HALO + tm:2 * HALO + tm] = jnp.where(it < nt - 1, pn_ref[0], 0.0)
    t = it * tm + lax.broadcasted_iota(jnp.int32, (tm, 1), 0)
    parts = []
    for g, w in enumerate(POOL_WINDOWS):
        hw = w // 2
        sl = slice(g * POOL_GC, (g + 1) * POOL_GC)
        acc = None
        for j in range(-hw, hw):
            piece = ext_ref[HALO + j:HALO + j + tm, sl]
            acc = piece if acc is None else acc + piece
        cnt = (jnp.minimum(t + hw, L) - jnp.maximum(t - hw, 0)).astype(F32)
        diff = (acc / cnt - u[:, sl]).astype(BF16)
        parts.append(jnp.dot(diff, pw_ref[g], preferred_element_type=F32))
    mixed = jnp.concatenate(parts, axis=1) * ps_ref[...]
    a_out = mixed * _silu(pz_ref[0])
    b_out = o_ref[0] * _silu(az_ref[0])
    cat = jnp.concatenate([a_out, b_out], axis=1).astype(BF16)
    out_ref[0] = x_ref[0] + jnp.dot(cat, wo_ref[...], preferred_element_type=F32)


def _even_out(pu, pz, o, az, x, pool_w, pool_scale, w_out):
    B, L, _ = x.shape
    tm = ROW_TILE
    nt = L // tm
    hb = tm // HALO
    full = lambda shape: pl.BlockSpec(shape, lambda b, i: (0,) * len(shape))
    tok = lambda w: pl.BlockSpec((1, tm, w), lambda b, i: (b, i, 0))
    return pl.pallas_call(
        _even_out_kernel,
        grid=(B, nt),
        in_specs=[
            tok(POOL_WIDTH),
            pl.BlockSpec((1, HALO, POOL_WIDTH), lambda b, i: (b, jnp.maximum(i * hb - 1, 0), 0)),
            pl.BlockSpec((1, HALO, POOL_WIDTH),
                         lambda b, i: (b, jnp.minimum((i + 1) * hb, L // HALO - 1), 0)),
            tok(POOL_WIDTH), tok(ATTN_WIDTH), tok(ATTN_WIDTH), tok(D_MODEL),
            full((len(POOL_WINDOWS), POOL_GC, POOL_GC)),
            full((1, POOL_WIDTH)),
            full((D_MODEL, D_MODEL)),
        ],
        out_specs=tok(D_MODEL),
        out_shape=jax.ShapeDtypeStruct((B, L, D_MODEL), F32),
        scratch_shapes=[pltpu.VMEM((tm + 2 * HALO, POOL_WIDTH), F32)],
        compiler_params=pltpu.CompilerParams(
            dimension_semantics=("arbitrary", "arbitrary"), vmem_limit_bytes=VMEM_LIMIT),
        name="even_out",
    )(pu, pu, pu, pz, o, az, x, pool_w, pool_scale, w_out)


def _odd_kernel(x_ref, ng_ref, wi_ref, sg_ref, ws_ref, bs_ref, wo_ref, out_ref, sv_ref):
    x = x_ref[0]
    tm = x.shape[0]
    h = _rms_rows(x, ng_ref[...]).astype(BF16)

    def proj(j):
        return jnp.dot(h, wi_ref[:, j * SGU_WIDTH:(j + 1) * SGU_WIDTH], preferred_element_type=F32)

    vv = _rms_rows(_gelu(proj(1)), sg_ref[...]).astype(BF16)
    nch = tm // CHUNK
    for g in range(SGU_GROUPS):
        gs = slice(g * SGU_GC, (g + 1) * SGU_GC)
        vcat = jnp.concatenate([vv[n * CHUNK:(n + 1) * CHUNK, gs] for n in range(nch)], axis=1)
        res = jnp.dot(ws_ref[g], vcat, preferred_element_type=F32)
        for n in range(nch):
            sv_ref[n * CHUNK:(n + 1) * CHUNK, gs] = res[:, n * SGU_GC:(n + 1) * SGU_GC] + bs_ref[g]
    y = (_gelu(proj(0)) * sv_ref[...] * _silu(proj(2))).astype(BF16)
    out_ref[0] = x + jnp.dot(y, wo_ref[...], preferred_element_type=F32)


def _odd_layer(x, norm_g, w_in, sgu_g, w_s, b_s, w_out):
    B, L, _ = x.shape
    tm = ROW_TILE
    full = lambda shape: pl.BlockSpec(shape, lambda b, i: (0,) * len(shape))
    tok = pl.BlockSpec((1, tm, D_MODEL), lambda b, i: (b, i, 0))
    return pl.pallas_call(
        _odd_kernel,
        grid=(B, L // tm),
        in_specs=[
            tok,
            full((1, D_MODEL)),
            full((D_MODEL, 3 * SGU_WIDTH)),
            full((1, SGU_WIDTH)),
            full((SGU_GROUPS, CHUNK, CHUNK)),
            full((SGU_GROUPS, CHUNK, SGU_GC)),
            full((SGU_WIDTH, D_MODEL)),
        ],
        out_specs=tok,
        out_shape=jax.ShapeDtypeStruct((B, L, D_MODEL), F32),
        scratch_shapes=[pltpu.VMEM((tm, SGU_WIDTH), F32)],
        compiler_params=pltpu.CompilerParams(
            dimension_semantics=("arbitrary", "arbitrary"), vmem_limit_bytes=VMEM_LIMIT),
        name="odd_layer",
    )(x, norm_g, w_in, sgu_g, w_s, b_s, w_out)


def _rope_tables(L):
    rows_n = L // GRID_W
    row = jnp.repeat(jnp.arange(rows_n), GRID_W).astype(F32)
    col = jnp.tile(jnp.arange(GRID_W), rows_n).astype(F32)
    inv = 1.0 / (ROPE_THETA ** (jnp.arange(ROPE_FREQ, dtype=F32) / ROPE_FREQ))
    ang_r = row[:, None] * inv[None, :]
    ang_c = col[:, None] * inv[None, :]
    cr, sr, cc, sc = jnp.cos(ang_r), jnp.sin(ang_r), jnp.cos(ang_c), jnp.sin(ang_c)
    cos_h = jnp.concatenate([cr, cr, cc, cc], axis=-1)
    sin_h = jnp.concatenate([-sr, sr, -sc, sc], axis=-1)
    reps = LANES // HEAD_DIM
    cos_tt = jnp.concatenate([cr, cc], axis=-1).T
    sin_tt = jnp.concatenate([sr, sc], axis=-1).T
    return jnp.tile(cos_h, (1, reps)), jnp.tile(sin_h, (1, reps)), cos_tt, sin_tt


def _head_mean_matrix():
    idx = np.arange(LANES) // HEAD_DIM
    return jnp.asarray((idx[:, None] == idx[None, :]).astype(np.float32) / HEAD_DIM, dtype=BF16)


def _trunk(x, params, tables, gmat):
    (norm_e, w_in_e, pool_w, pool_scale, q_norm, k_norm, w_out_e,
     norm_o, w_in_o, sgu_norm, w_s, b_s, w_out_o) = params
    for i in range(DEPTH):
        j = i // 2
        if i % 2 == 0:
            q_g = jnp.broadcast_to(jnp.tile(q_norm[j], N_Q_HEADS)[:, None], (ATTN_WIDTH, ROW_TILE))
            pu, pz, az, qt, k, vx = _even_in(
                x, norm_e[j][None, :], w_in_e[j], q_g, jnp.tile(k_norm[j], N_KV_HEADS)[None, :],
                tables[0], tables[1], gmat, tables[2], tables[3])
            o = _attention(qt, k, vx)
            x = _even_out(pu, pz, o, az, x, pool_w[j], pool_scale[j][None, :], w_out_e[j])
        else:
            bs_b = jnp.broadcast_to(b_s[j][:, :, None], (SGU_GROUPS, CHUNK, SGU_GC))
            x = _odd_layer(x, norm_o[j][None, :], w_in_o[j], sgu_norm[j][None, :], w_s[j], bs_b,
                           w_out_o[j])
    return x


def kernel(x_prompt, x_sample, norm_e, w_in_e, pool_w, pool_scale, q_norm, k_norm, w_out_e,
           norm_o, w_in_o, sgu_norm, w_s, b_s, w_out_o):
    params = (norm_e, w_in_e.astype(BF16), pool_w.astype(BF16), pool_scale, q_norm, k_norm,
              w_out_e.astype(BF16), norm_o, w_in_o.astype(BF16), sgu_norm, w_s.astype(BF16), b_s,
              w_out_o.astype(BF16))
    tables = _rope_tables(x_prompt.shape[1])
    gmat = _head_mean_matrix()
    y_prompt = _trunk(x_prompt, params, tables, gmat)
    y_sample = _trunk(x_sample, params, tables, gmat)
    return (y_prompt, y_sample)
```

```python
import math

import jax
import jax.numpy as jnp
import numpy as np
from jax import lax
from jax.experimental import pallas as pl
from jax.experimental.pallas import tpu as pltpu

F32 = jnp.float32
BF16 = jnp.bfloat16

D_MODEL = 1024
DEPTH = 4
GRID_W = 64
EPS = 1e-6

POOL_WIDTH = 512
POOL_WINDOWS = (2, 4, 8, 16)
POOL_GC = 128

HEAD_DIM = 64
N_Q_HEADS = 8
N_KV_HEADS = 2
Q_PER_KV = 4
ATTN_WIDTH = 512
KV_WIDTH = 128
ROPE_FREQ = 16
ROPE_THETA = 10000.0

OFF_PU, OFF_PZ, OFF_Q, OFF_K, OFF_V, OFF_AZ, EVEN_IN = 0, 512, 1024, 1536, 1664, 1792, 2304

SGU_WIDTH = 1024
SGU_GROUPS = 8
SGU_GC = 128
CHUNK = 128

LANES = 128
HALO = 8

ROW_TILE = 512
Q_TILE = 128
KEY_TILE = 256
ONES_ROWS = 16
Q_PRESCALE = HEAD_DIM ** -0.5 * math.log2(math.e)

PIPE_DEPTH = 3

VMEM_LIMIT = 56 * 1024 * 1024


def _silu(x):
    return x * (1.0 / (1.0 + jnp.exp(-x)))


def _gelu(x):
    return 0.5 * x * (1.0 + lax.erf(x * np.float32(math.sqrt(0.5))))


def _rms_rows(x, g):
    ms = jnp.mean(x * x, axis=-1, keepdims=True)
    return x * lax.rsqrt(ms + EPS) * g


def _norm_rope_t(t, gain, scale, cos_r, sin_r, cos_c, sin_c):
    ms = jnp.sum(t * t, axis=0, keepdims=True) * np.float32(1.0 / HEAD_DIM)
    y = t * (lax.rsqrt(ms + EPS) * np.float32(scale)) * gain
    x1, x2, x3, x4 = (y[j * ROPE_FREQ:(j + 1) * ROPE_FREQ] for j in range(4))
    return jnp.concatenate([x1 * cos_r - x2 * sin_r, x2 * cos_r + x1 * sin_r,
                            x3 * cos_c - x4 * sin_c, x4 * cos_c + x3 * sin_c], axis=0)


def _even_in_kernel(x_ref, xp_ref, xn_ref, ng_ref, w_ref, qg_ref, kg_ref, cost_ref, sint_ref,
                    pw_ref, ps_ref, a_ref, az_ref, qt_ref, k_ref, vx_ref, h_ref, ext_ref):
    it = pl.program_id(1)
    nt = pl.num_programs(1)
    tm = x_ref.shape[1]
    L = tm * nt
    ng = ng_ref[...]
    h_ref[0:tm] = _rms_rows(x_ref[0], ng).astype(BF16)
    h_ref[tm:tm + 2 * HALO] = _rms_rows(
        jnp.concatenate([xp_ref[0], xn_ref[0]], axis=0), ng).astype(BF16)

    def proj(lo, hi):
        return jnp.dot(h_ref[0:tm], w_ref[:, lo:hi], preferred_element_type=F32)

    u_all = jnp.dot(h_ref[...], w_ref[:, OFF_PU:OFF_PZ], preferred_element_type=F32)
    u = u_all[0:tm]
    ext_ref[0:HALO] = jnp.where(it > 0, u_all[tm:tm + HALO], 0.0)
    ext_ref[HALO:HALO + tm] = u
    ext_ref[HALO + tm:2 * HALO + tm] = jnp.where(it < nt - 1, u_all[tm + HALO:tm + 2 * HALO], 0.0)
    t = it * tm + lax.broadcasted_iota(jnp.int32, (tm, 1), 0)
    parts = []
    for g, w in enumerate(POOL_WINDOWS):
        hw = w // 2
        sl = slice(g * POOL_GC, (g + 1) * POOL_GC)
        acc = None
        for j in range(-hw, hw):
            piece = ext_ref[HALO + j:HALO + j + tm, sl]
            acc = piece if acc is None else acc + piece
        cnt = (jnp.minimum(t + hw, L) - jnp.maximum(t - hw, 0)).astype(F32)
        parts.append((acc / cnt - u[:, sl]).astype(BF16))

    cos_r, cos_c = cost_ref[0:ROPE_FREQ], cost_ref[ROPE_FREQ:2 * ROPE_FREQ]
    sin_r, sin_c = sint_ref[0:ROPE_FREQ], sint_ref[ROPE_FREQ:2 * ROPE_FREQ]
    qt = proj(OFF_Q, OFF_K).T
    for hd in range(N_Q_HEADS):
        rows = slice(hd * HEAD_DIM, (hd + 1) * HEAD_DIM)
        qt_ref[0, rows, :] = _norm_rope_t(qt[rows], qg_ref[rows], Q_PRESCALE,
                                          cos_r, sin_r, cos_c, sin_c).astype(BF16)
    kt = proj(OFF_K, OFF_V).T
    kt = jnp.concatenate(
        [_norm_rope_t(kt[kv * HEAD_DIM:(kv + 1) * HEAD_DIM], kg_ref[kv * HEAD_DIM:(kv + 1) * HEAD_DIM],
                      1.0, cos_r, sin_r, cos_c, sin_c) for kv in range(N_KV_HEADS)], axis=0)
    k_ref[0] = kt.T.astype(BF16)

    vt = proj(OFF_V, OFF_AZ).T
    ones = jnp.ones((ONES_ROWS, tm), BF16)
    for kv in range(N_KV_HEADS):
        vx_ref[0, kv, 0:HEAD_DIM] = vt[kv * HEAD_DIM:(kv + 1) * HEAD_DIM].astype(BF16)
        vx_ref[0, kv, HEAD_DIM:HEAD_DIM + ONES_ROWS] = ones

    gate = _silu(proj(OFF_PZ, OFF_Q))
    mixed = jnp.concatenate([jnp.dot(parts[g], pw_ref[g], preferred_element_type=F32)
                             for g in range(len(POOL_WINDOWS))], axis=1)
    a_ref[0] = (mixed * ps_ref[...] * gate).astype(BF16)
    az_ref[0] = proj(OFF_AZ, EVEN_IN)


def _even_in(x, norm_g, w_in, q_g, k_g, cos_tt, sin_tt, pool_w, pool_scale):
    B, L, _ = x.shape
    tm = ROW_TILE
    nt = L // tm
    hb = tm // HALO
    full = lambda shape: pl.BlockSpec(shape, lambda b, i: (0,) * len(shape))
    tok = lambda w: pl.BlockSpec((1, tm, w), lambda b, i: (b, i, 0))
    vrows = HEAD_DIM + ONES_ROWS
    return pl.pallas_call(
        _even_in_kernel,
        grid=(B, nt),
        in_specs=[
            tok(D_MODEL),
            pl.BlockSpec((1, HALO, D_MODEL), lambda b, i: (b, jnp.maximum(i * hb - 1, 0), 0)),
            pl.BlockSpec((1, HALO, D_MODEL),
                         lambda b, i: (b, jnp.minimum((i + 1) * hb, L // HALO - 1), 0)),
            full((1, D_MODEL)),
            full((D_MODEL, EVEN_IN)),
            full((ATTN_WIDTH, tm)),
            full((KV_WIDTH, tm)),
            pl.BlockSpec((2 * ROPE_FREQ, tm), lambda b, i: (0, i)),
            pl.BlockSpec((2 * ROPE_FREQ, tm), lambda b, i: (0, i)),
            full((len(POOL_WINDOWS), POOL_GC, POOL_GC)),
            full((1, POOL_WIDTH)),
        ],
        out_specs=[tok(POOL_WIDTH), tok(ATTN_WIDTH),
                   pl.BlockSpec((1, ATTN_WIDTH, tm), lambda b, i: (b, 0, i)),
                   tok(KV_WIDTH),
                   pl.BlockSpec((1, N_KV_HEADS, vrows, tm), lambda b, i: (b, 0, 0, i))],
        out_shape=[
            jax.ShapeDtypeStruct((B, L, POOL_WIDTH), BF16),
            jax.ShapeDtypeStruct((B, L, ATTN_WIDTH), F32),
            jax.ShapeDtypeStruct((B, ATTN_WIDTH, L), BF16),
            jax.ShapeDtypeStruct((B, L, KV_WIDTH), BF16),
            jax.ShapeDtypeStruct((B, N_KV_HEADS, vrows, L), BF16),
        ],
        scratch_shapes=[pltpu.VMEM((tm + 2 * HALO, D_MODEL), BF16),
                        pltpu.VMEM((tm + 2 * HALO, POOL_WIDTH), F32)],
        compiler_params=pltpu.CompilerParams(
            dimension_semantics=("arbitrary", "arbitrary"), vmem_limit_bytes=VMEM_LIMIT),
        name="even_in",
    )(x, x, x, norm_g, w_in, q_g, k_g, cos_tt, sin_tt, pool_w, pool_scale)


def _attn_kernel(qt_ref, k_ref, vx_ref, az_ref, o_ref, s_ref, m_ref, qx_ref, acc_ref):
    i = pl.program_id(0)
    n_tiles = pl.num_programs(0) - PIPE_DEPTH
    L = k_ref.shape[1]
    tq = qt_ref.shape[2]
    nq = L // tq
    nc = L // KEY_TILE
    R = Q_PER_KV * tq

    @pl.when(i == 0)
    def _():
        s_ref[...] = jnp.zeros(s_ref.shape, F32)
        m_ref[...] = jnp.zeros(m_ref.shape, F32)
        qx_ref[...] = jnp.zeros(qx_ref.shape, BF16)
        acc_ref[...] = jnp.ones(acc_ref.shape, F32)

    acc = acc_ref[...]
    out = acc[0:HEAD_DIM] * (1.0 / acc[HEAD_DIM:HEAD_DIM + 1])
    gate = _silu(az_ref[0])
    for pair in range(Q_PER_KV // 2):
        two = jnp.concatenate([out[:, (2 * pair) * tq:(2 * pair + 1) * tq],
                               out[:, (2 * pair + 1) * tq:(2 * pair + 2) * tq]], axis=0)
        lanes = slice(pair * LANES, (pair + 1) * LANES)
        o_ref[0, :, lanes] = (two.T * gate[:, lanes]).astype(BF16)

    qx = qx_ref[...]
    m_prev = jnp.max(m_ref[...], axis=0, keepdims=True)
    m_new = None
    acc = None
    for c in range(nc):
        keys = slice(c * KEY_TILE, (c + 1) * KEY_TILE)
        s_c = jnp.dot(k_ref[0, keys, :], qx, preferred_element_type=F32)
        p = jnp.exp2(s_ref[c] - m_prev).astype(BF16)
        d = jnp.dot(vx_ref[0, 0, :, keys], p, preferred_element_type=F32)
        acc = d if acc is None else acc + d
        s_ref[c] = s_c
        mc = jnp.max(s_c.reshape(KEY_TILE // HALO, HALO, R), axis=0)
        m_new = mc if m_new is None else jnp.maximum(m_new, mc)
    m_ref[...] = m_new
    acc_ref[...] = acc

    kvh = lax.shift_right_logical(jnp.minimum(i, n_tiles - 1), nq.bit_length() - 1) & 1
    qt = qt_ref[0]
    qcat = jnp.concatenate(
        [qt[g * HEAD_DIM:(g + 1) * HEAD_DIM, :] for g in range(Q_PER_KV)], axis=1)
    zero = jnp.zeros_like(qcat)
    qx_ref[0:HEAD_DIM] = jnp.where(kvh == 0, qcat, zero)
    qx_ref[HEAD_DIM:2 * HEAD_DIM] = jnp.where(kvh == 1, qcat, zero)


def _attention(qt, k, vx, az):
    B, _, L = qt.shape
    tq = Q_TILE
    nq = L // tq
    gw = Q_PER_KV * HEAD_DIM
    n_tiles = B * N_KV_HEADS * nq

    nq_bits = nq.bit_length() - 1
    assert nq == 1 << nq_bits and N_KV_HEADS == 2

    def tile(i, lag):
        t = jnp.clip(i - lag, 0, n_tiles - 1)
        return (lax.shift_right_logical(t, nq_bits + 1), lax.shift_right_logical(t, nq_bits) & 1,
                t & (nq - 1))

    def q_map(i):
        b, h, iq = tile(i, 0)
        return (b, h, iq)

    def k_map(i):
        return (tile(i, 1)[0], 0, 0)

    def v_map(i):
        b, h, _ = tile(i, 2)
        return (b, h, 0, 0)

    def o_map(i):
        b, h, iq = tile(i, 3)
        return (b, iq, h)

    R = Q_PER_KV * tq
    return pl.pallas_call(
        _attn_kernel,
        grid=(n_tiles + PIPE_DEPTH,),
        in_specs=[
            pl.BlockSpec((1, gw, tq), q_map),
            pl.BlockSpec((1, L, KV_WIDTH), k_map),
            pl.BlockSpec((1, 1, HEAD_DIM + ONES_ROWS, L), v_map),
            pl.BlockSpec((1, tq, gw), o_map),
        ],
        out_specs=pl.BlockSpec((1, tq, gw), o_map),
        out_shape=jax.ShapeDtypeStruct((B, L, ATTN_WIDTH), BF16),
        scratch_shapes=[pltpu.VMEM((L // KEY_TILE, KEY_TILE, R), F32),
                        pltpu.VMEM((HALO, R), F32),
                        pltpu.VMEM((2 * HEAD_DIM, R), BF16),
                        pltpu.VMEM((HEAD_DIM + ONES_ROWS, R), F32)],
        compiler_params=pltpu.CompilerParams(
            dimension_semantics=("arbitrary",), vmem_limit_bytes=VMEM_LIMIT),
        name="attention",
    )(qt, k, vx, az)


def _out_odd_kernel(a_ref, b_ref, x_ref, woe_ref, ng_ref, wi_ref, sg_ref, ws_ref, bs_ref, wo_ref,
                    out_ref, sv_ref):
    tm = x_ref.shape[1]
    cat = jnp.concatenate([a_ref[0], b_ref[0]], axis=1)
    x = x_ref[0] + jnp.dot(cat, woe_ref[...], preferred_element_type=F32)
    h = _rms_rows(x, ng_ref[...]).astype(BF16)

    def proj(j):
        return jnp.dot(h, wi_ref[:, j * SGU_WIDTH:(j + 1) * SGU_WIDTH], preferred_element_type=F32)

    vv = _rms_rows(_gelu(proj(1)), sg_ref[...]).astype(BF16)
    nch = tm // CHUNK
    for g in range(SGU_GROUPS):
        gs = slice(g * SGU_GC, (g + 1) * SGU_GC)
        vcat = jnp.concatenate([vv[n * CHUNK:(n + 1) * CHUNK, gs] for n in range(nch)], axis=1)
        res = jnp.dot(ws_ref[g], vcat, preferred_element_type=F32)
        for n in range(nch):
            sv_ref[n * CHUNK:(n + 1) * CHUNK, gs] = res[:, n * SGU_GC:(n + 1) * SGU_GC] + bs_ref[g]
    y = (_gelu(proj(0)) * sv_ref[...] * _silu(proj(2))).astype(BF16)
    out_ref[0] = x + jnp.dot(y, wo_ref[...], preferred_element_type=F32)


def _out_odd(a, b, x, w_out_e, norm_g, w_in, sgu_g, w_s, b_s, w_out):
    B, L, _ = x.shape
    tm = ROW_TILE
    full = lambda shape: pl.BlockSpec(shape, lambda bb, i: (0,) * len(shape))
    tok = lambda w: pl.BlockSpec((1, tm, w), lambda bb, i: (bb, i, 0))
    return pl.pallas_call(
        _out_odd_kernel,
        grid=(B, L // tm),
        in_specs=[
            tok(POOL_WIDTH), tok(ATTN_WIDTH), tok(D_MODEL),
            full((D_MODEL, D_MODEL)),
            full((1, D_MODEL)),
            full((D_MODEL, 3 * SGU_WIDTH)),
            full((1, SGU_WIDTH)),
            full((SGU_GROUPS, CHUNK, CHUNK)),
            full((SGU_GROUPS, CHUNK, SGU_GC)),
            full((SGU_WIDTH, D_MODEL)),
        ],
        out_specs=tok(D_MODEL),
        out_shape=jax.ShapeDtypeStruct((B, L, D_MODEL), F32),
        scratch_shapes=[pltpu.VMEM((tm, SGU_WIDTH), F32)],
        compiler_params=pltpu.CompilerParams(
            dimension_semantics=("arbitrary", "arbitrary"), vmem_limit_bytes=VMEM_LIMIT),
        name="out_odd",
    )(a, b, x, w_out_e, norm_g, w_in, sgu_g, w_s, b_s, w_out)


def _rope_tables(L):
    rows_n = L // GRID_W
    row = jnp.repeat(jnp.arange(rows_n), GRID_W).astype(F32)
    col = jnp.tile(jnp.arange(GRID_W), rows_n).astype(F32)
    inv = 1.0 / (ROPE_THETA ** (jnp.arange(ROPE_FREQ, dtype=F32) / ROPE_FREQ))
    ang_r = row[:, None] * inv[None, :]
    ang_c = col[:, None] * inv[None, :]
    cos_tt = jnp.concatenate([jnp.cos(ang_r), jnp.cos(ang_c)], axis=-1).T
    sin_tt = jnp.concatenate([jnp.sin(ang_r), jnp.sin(ang_c)], axis=-1).T
    return cos_tt, sin_tt


def _trunk(x, params, tables):
    (norm_e, w_in_e, pool_w, pool_scale, q_norm, k_norm, w_out_e,
     norm_o, w_in_o, sgu_norm, w_s, b_s, w_out_o) = params
    for j in range(DEPTH // 2):
        q_g = jnp.broadcast_to(jnp.tile(q_norm[j], N_Q_HEADS)[:, None], (ATTN_WIDTH, ROW_TILE))
        k_g = jnp.broadcast_to(jnp.tile(k_norm[j], N_KV_HEADS)[:, None], (KV_WIDTH, ROW_TILE))
        a, az, qt, k, vx = _even_in(x, norm_e[j][None, :], w_in_e[j], q_g, k_g, tables[0], tables[1],
                                    pool_w[j], pool_scale[j][None, :])
        b = _attention(qt, k, vx, az)
        bs_b = jnp.broadcast_to(b_s[j][:, :, None], (SGU_GROUPS, CHUNK, SGU_GC))
        x = _out_odd(a, b, x, w_out_e[j], norm_o[j][None, :], w_in_o[j], sgu_norm[j][None, :],
                     w_s[j], bs_b, w_out_o[j])
    return x


def kernel(x_prompt, x_sample, norm_e, w_in_e, pool_w, pool_scale, q_norm, k_norm, w_out_e,
           norm_o, w_in_o, sgu_norm, w_s, b_s, w_out_o):
    params = (norm_e, w_in_e.astype(BF16), pool_w.astype(BF16), pool_scale, q_norm, k_norm,
              w_out_e.astype(BF16), norm_o, w_in_o.astype(BF16), sgu_norm, w_s.astype(BF16), b_s,
              w_out_o.astype(BF16))
    tables = _rope_tables(x_prompt.shape[1])
    y_prompt = _trunk(x_prompt, params, tables)
    y_sample = _trunk(x_sample, params, tables)
    return (y_prompt, y_sample)
```

```python
import math

import jax
import jax.numpy as jnp
import numpy as np
from jax import lax
from jax.experimental import pallas as pl
from jax.experimental.pallas import tpu as pltpu

F32 = jnp.float32
BF16 = jnp.bfloat16

D_MODEL = 1024
DEPTH = 4
GRID_W = 64
EPS = 1e-6

POOL_WIDTH = 512
POOL_WINDOWS = (2, 4, 8, 16)
POOL_GC = 128

HEAD_DIM = 64
N_Q_HEADS = 8
N_KV_HEADS = 2
Q_PER_KV = 4
ATTN_WIDTH = 512
KV_WIDTH = 128
ROPE_FREQ = 16
ROPE_THETA = 10000.0

OFF_PU, OFF_PZ, OFF_Q, OFF_K, OFF_V, OFF_AZ, EVEN_IN = 0, 512, 1024, 1536, 1664, 1792, 2304

SGU_WIDTH = 1024
SGU_GROUPS = 8
SGU_GC = 128
CHUNK = 128

LANES = 128
HALO = 8

ROW_TILE = 512
Q_TILE = 128
KEY_TILE = 256
ONES_ROWS = 16
Q_PRESCALE = HEAD_DIM ** -0.5 * math.log2(math.e)

PIPE_DEPTH = 3

VMEM_LIMIT = 56 * 1024 * 1024


def _silu(x):
    return x * (1.0 / (1.0 + jnp.exp(-x)))


def _gelu(x):
    return 0.5 * x * (1.0 + lax.erf(x * np.float32(math.sqrt(0.5))))


def _rms_rows(x, g):
    ms = jnp.mean(x * x, axis=-1, keepdims=True)
    return x * lax.rsqrt(ms + EPS) * g


def _norm_rope_t(t, gain, scale, cos_r, sin_r, cos_c, sin_c):
    ms = jnp.sum(t * t, axis=0, keepdims=True) * np.float32(1.0 / HEAD_DIM)
    y = t * (lax.rsqrt(ms + EPS) * np.float32(scale)) * gain
    x1, x2, x3, x4 = (y[j * ROPE_FREQ:(j + 1) * ROPE_FREQ] for j in range(4))
    return jnp.concatenate([x1 * cos_r - x2 * sin_r, x2 * cos_r + x1 * sin_r,
                            x3 * cos_c - x4 * sin_c, x4 * cos_c + x3 * sin_c], axis=0)


def _even_in_kernel(x_ref, xp_ref, xn_ref, ng_ref, w_ref, qg_ref, kg_ref, cost_ref, sint_ref,
                    pw_ref, ps_ref, a_ref, az_ref, qt_ref, k_ref, vx_ref, h_ref, ext_ref):
    it = pl.program_id(1)
    nt = pl.num_programs(1)
    tm = x_ref.shape[1]
    L = tm * nt
    ng = ng_ref[...]
    h_ref[0:tm] = _rms_rows(x_ref[0], ng).astype(BF16)
    h_ref[tm:tm + 2 * HALO] = _rms_rows(
        jnp.concatenate([xp_ref[0], xn_ref[0]], axis=0), ng).astype(BF16)

    def proj(lo, hi):
        return jnp.dot(h_ref[0:tm], w_ref[:, lo:hi], preferred_element_type=F32)

    u_all = jnp.dot(h_ref[...], w_ref[:, OFF_PU:OFF_PZ], preferred_element_type=F32)
    u = u_all[0:tm]
    ext_ref[0:HALO] = jnp.where(it > 0, u_all[tm:tm + HALO], 0.0)
    ext_ref[HALO:HALO + tm] = u
    ext_ref[HALO + tm:2 * HALO + tm] = jnp.where(it < nt - 1, u_all[tm + HALO:tm + 2 * HALO], 0.0)
    t = it * tm + lax.broadcasted_iota(jnp.int32, (tm, 1), 0)
    parts = []
    for g, w in enumerate(POOL_WINDOWS):
        hw = w // 2
        sl = slice(g * POOL_GC, (g + 1) * POOL_GC)
        acc = None
        for j in range(-hw, hw):
            piece = ext_ref[HALO + j:HALO + j + tm, sl]
            acc = piece if acc is None else acc + piece
        cnt = (jnp.minimum(t + hw, L) - jnp.maximum(t - hw, 0)).astype(F32)
        parts.append((acc / cnt - u[:, sl]).astype(BF16))

    cos_r, cos_c = cost_ref[0:ROPE_FREQ], cost_ref[ROPE_FREQ:2 * ROPE_FREQ]
    sin_r, sin_c = sint_ref[0:ROPE_FREQ], sint_ref[ROPE_FREQ:2 * ROPE_FREQ]
    qt = proj(OFF_Q, OFF_K).T
    for hd in range(N_Q_HEADS):
        rows = slice(hd * HEAD_DIM, (hd + 1) * HEAD_DIM)
        qt_ref[0, rows, :] = _norm_rope_t(qt[rows], qg_ref[rows], Q_PRESCALE,
                                          cos_r, sin_r, cos_c, sin_c).astype(BF16)
    kt = proj(OFF_K, OFF_V).T
    kt = jnp.concatenate(
        [_norm_rope_t(kt[kv * HEAD_DIM:(kv + 1) * HEAD_DIM], kg_ref[kv * HEAD_DIM:(kv + 1) * HEAD_DIM],
                      1.0, cos_r, sin_r, cos_c, sin_c) for kv in range(N_KV_HEADS)], axis=0)
    k_ref[0] = kt.T.astype(BF16)

    vt = proj(OFF_V, OFF_AZ).T
    ones = jnp.ones((ONES_ROWS, tm), BF16)
    for kv in range(N_KV_HEADS):
        vx_ref[0, kv, 0:HEAD_DIM] = vt[kv * HEAD_DIM:(kv + 1) * HEAD_DIM].astype(BF16)
        vx_ref[0, kv, HEAD_DIM:HEAD_DIM + ONES_ROWS] = ones

    gate = _silu(proj(OFF_PZ, OFF_Q))
    mixed = jnp.concatenate([jnp.dot(parts[g], pw_ref[g], preferred_element_type=F32)
                             for g in range(len(POOL_WINDOWS))], axis=1)
    a_ref[0] = (mixed * ps_ref[...] * gate).astype(BF16)
    az_ref[0] = proj(OFF_AZ, EVEN_IN)


def _even_in(x, norm_g, w_in, q_g, k_g, cos_tt, sin_tt, pool_w, pool_scale):
    B, L, _ = x.shape
    tm = ROW_TILE
    nt = L // tm
    hb = tm // HALO
    full = lambda shape: pl.BlockSpec(shape, lambda b, i: (0,) * len(shape))
    tok = lambda w: pl.BlockSpec((1, tm, w), lambda b, i: (b, i, 0))
    vrows = HEAD_DIM + ONES_ROWS
    return pl.pallas_call(
        _even_in_kernel,
        grid=(B, nt),
        in_specs=[
            tok(D_MODEL),
            pl.BlockSpec((1, HALO, D_MODEL), lambda b, i: (b, jnp.maximum(i * hb - 1, 0), 0)),
            pl.BlockSpec((1, HALO, D_MODEL),
                         lambda b, i: (b, jnp.minimum((i + 1) * hb, L // HALO - 1), 0)),
            full((1, D_MODEL)),
            full((D_MODEL, EVEN_IN)),
            full((ATTN_WIDTH, tm)),
            full((KV_WIDTH, tm)),
            pl.BlockSpec((2 * ROPE_FREQ, tm), lambda b, i: (0, i)),
            pl.BlockSpec((2 * ROPE_FREQ, tm), lambda b, i: (0, i)),
            full((len(POOL_WINDOWS), POOL_GC, POOL_GC)),
            full((1, POOL_WIDTH)),
        ],
        out_specs=[tok(POOL_WIDTH), tok(ATTN_WIDTH),
                   pl.BlockSpec((1, ATTN_WIDTH, tm), lambda b, i: (b, 0, i)),
                   tok(KV_WIDTH),
                   pl.BlockSpec((1, N_KV_HEADS, vrows, tm), lambda b, i: (b, 0, 0, i))],
        out_shape=[
            jax.ShapeDtypeStruct((B, L, POOL_WIDTH), BF16),
            jax.ShapeDtypeStruct((B, L, ATTN_WIDTH), F32),
            jax.ShapeDtypeStruct((B, ATTN_WIDTH, L), BF16),
            jax.ShapeDtypeStruct((B, L, KV_WIDTH), BF16),
            jax.ShapeDtypeStruct((B, N_KV_HEADS, vrows, L), BF16),
        ],
        scratch_shapes=[pltpu.VMEM((tm + 2 * HALO, D_MODEL), BF16),
                        pltpu.VMEM((tm + 2 * HALO, POOL_WIDTH), F32)],
        compiler_params=pltpu.CompilerParams(
            dimension_semantics=("arbitrary", "arbitrary"), vmem_limit_bytes=VMEM_LIMIT),
        name="even_in",
    )(x, x, x, norm_g, w_in, q_g, k_g, cos_tt, sin_tt, pool_w, pool_scale)


def _attn_kernel(qt_ref, k_ref, vx_ref, az_ref, o_ref, s_ref, m_ref):
    i = pl.program_id(0)
    n_tiles = pl.num_programs(0) - 1
    L = k_ref.shape[1]
    tq = qt_ref.shape[2]
    nq = L // tq
    nc = L // KEY_TILE
    R = Q_PER_KV * tq

    @pl.when(i == 0)
    def _():
        s_ref[...] = jnp.zeros(s_ref.shape, F32)
        m_ref[...] = jnp.zeros(m_ref.shape, F32)

    kvh = (jnp.minimum(i, n_tiles - 1) // nq) % N_KV_HEADS
    qt = qt_ref[0]
    qcat = jnp.concatenate(
        [qt[g * HEAD_DIM:(g + 1) * HEAD_DIM, :] for g in range(Q_PER_KV)], axis=1)
    zero = jnp.zeros_like(qcat)
    qx = jnp.concatenate([jnp.where(kvh == 0, qcat, zero),
                          jnp.where(kvh == 1, qcat, zero)], axis=0)
    m_prev = jnp.max(m_ref[...], axis=0, keepdims=True)
    m_new = None
    acc = None
    for c in range(nc):
        keys = slice(c * KEY_TILE, (c + 1) * KEY_TILE)
        s_c = jnp.dot(k_ref[0, keys, :], qx, preferred_element_type=F32)
        p = jnp.exp2(s_ref[c] - m_prev).astype(BF16)
        d = jnp.dot(vx_ref[0, 0, :, keys], p, preferred_element_type=F32)
        acc = d if acc is None else acc + d
        s_ref[c] = s_c
        mc = jnp.max(s_c.reshape(KEY_TILE // HALO, HALO, R), axis=0)
        m_new = mc if m_new is None else jnp.maximum(m_new, mc)
    m_ref[...] = m_new
    out = acc[0:HEAD_DIM] * (1.0 / acc[HEAD_DIM:HEAD_DIM + 1])
    gate = _silu(az_ref[0])
    for pair in range(Q_PER_KV // 2):
        two = jnp.concatenate([out[:, (2 * pair) * tq:(2 * pair + 1) * tq],
                               out[:, (2 * pair + 1) * tq:(2 * pair + 2) * tq]], axis=0)
        lanes = slice(pair * LANES, (pair + 1) * LANES)
        o_ref[0, :, lanes] = (two.T * gate[:, lanes]).astype(BF16)


def _attention(qt, k, vx, az):
    B, _, L = qt.shape
    tq = Q_TILE
    nq = L // tq
    gw = Q_PER_KV * HEAD_DIM
    n_tiles = B * N_KV_HEADS * nq

    def tile(t):
        return t // (N_KV_HEADS * nq), (t // nq) % N_KV_HEADS, t % nq

    def cur(i):
        return tile(jnp.minimum(i, n_tiles - 1))

    def prev(i):
        return tile(jnp.maximum(i - 1, 0))

    def q_map(i):
        b, h, iq = cur(i)
        return (b, h, iq)

    def k_map(i):
        return (cur(i)[0], 0, 0)

    def v_map(i):
        b, h, _ = prev(i)
        return (b, h, 0, 0)

    def o_map(i):
        b, h, iq = prev(i)
        return (b, iq, h)

    return pl.pallas_call(
        _attn_kernel,
        grid=(n_tiles + 1,),
        in_specs=[
            pl.BlockSpec((1, gw, tq), q_map),
            pl.BlockSpec((1, L, KV_WIDTH), k_map),
            pl.BlockSpec((1, 1, HEAD_DIM + ONES_ROWS, L), v_map),
            pl.BlockSpec((1, tq, gw), o_map),
        ],
        out_specs=pl.BlockSpec((1, tq, gw), o_map),
        out_shape=jax.ShapeDtypeStruct((B, L, ATTN_WIDTH), BF16),
        scratch_shapes=[pltpu.VMEM((L // KEY_TILE, KEY_TILE, Q_PER_KV * tq), F32),
                        pltpu.VMEM((HALO, Q_PER_KV * tq), F32)],
        compiler_params=pltpu.CompilerParams(
            dimension_semantics=("arbitrary",), vmem_limit_bytes=VMEM_LIMIT),
        name="attention",
    )(qt, k, vx, az)


def _out_odd_kernel(a_ref, b_ref, x_ref, woe_ref, ng_ref, wi_ref, sg_ref, ws_ref, bs_ref, wo_ref,
                    out_ref, sv_ref):
    tm = x_ref.shape[1]
    cat = jnp.concatenate([a_ref[0], b_ref[0]], axis=1)
    x = x_ref[0] + jnp.dot(cat, woe_ref[...], preferred_element_type=F32)
    h = _rms_rows(x, ng_ref[...]).astype(BF16)

    def proj(j):
        return jnp.dot(h, wi_ref[:, j * SGU_WIDTH:(j + 1) * SGU_WIDTH], preferred_element_type=F32)

    vv = _rms_rows(_gelu(proj(1)), sg_ref[...]).astype(BF16)
    nch = tm // CHUNK
    for g in range(SGU_GROUPS):
        gs = slice(g * SGU_GC, (g + 1) * SGU_GC)
        vcat = jnp.concatenate([vv[n * CHUNK:(n + 1) * CHUNK, gs] for n in range(nch)], axis=1)
        res = jnp.dot(ws_ref[g], vcat, preferred_element_type=F32)
        for n in range(nch):
            sv_ref[n * CHUNK:(n + 1) * CHUNK, gs] = res[:, n * SGU_GC:(n + 1) * SGU_GC] + bs_ref[g]
    y = (_gelu(proj(0)) * sv_ref[...] * _silu(proj(2))).astype(BF16)
    out_ref[0] = x + jnp.dot(y, wo_ref[...], preferred_element_type=F32)


def _out_odd(a, b, x, w_out_e, norm_g, w_in, sgu_g, w_s, b_s, w_out):
    B, L, _ = x.shape
    tm = ROW_TILE
    full = lambda shape: pl.BlockSpec(shape, lambda bb, i: (0,) * len(shape))
    tok = lambda w: pl.BlockSpec((1, tm, w), lambda bb, i: (bb, i, 0))
    return pl.pallas_call(
        _out_odd_kernel,
        grid=(B, L // tm),
        in_specs=[
            tok(POOL_WIDTH), tok(ATTN_WIDTH), tok(D_MODEL),
            full((D_MODEL, D_MODEL)),
            full((1, D_MODEL)),
            full((D_MODEL, 3 * SGU_WIDTH)),
            full((1, SGU_WIDTH)),
            full((SGU_GROUPS, CHUNK, CHUNK)),
            full((SGU_GROUPS, CHUNK, SGU_GC)),
            full((SGU_WIDTH, D_MODEL)),
        ],
        out_specs=tok(D_MODEL),
        out_shape=jax.ShapeDtypeStruct((B, L, D_MODEL), F32),
        scratch_shapes=[pltpu.VMEM((tm, SGU_WIDTH), F32)],
        compiler_params=pltpu.CompilerParams(
            dimension_semantics=("arbitrary", "arbitrary"), vmem_limit_bytes=VMEM_LIMIT),
        name="out_odd",
    )(a, b, x, w_out_e, norm_g, w_in, sgu_g, w_s, b_s, w_out)


def _rope_tables(L):
    rows_n = L // GRID_W
    row = jnp.repeat(jnp.arange(rows_n), GRID_W).astype(F32)
    col = jnp.tile(jnp.arange(GRID_W), rows_n).astype(F32)
    inv = 1.0 / (ROPE_THETA ** (jnp.arange(ROPE_FREQ, dtype=F32) / ROPE_FREQ))
    ang_r = row[:, None] * inv[None, :]
    ang_c = col[:, None] * inv[None, :]
    cos_tt = jnp.concatenate([jnp.cos(ang_r), jnp.cos(ang_c)], axis=-1).T
    sin_tt = jnp.concatenate([jnp.sin(ang_r), jnp.sin(ang_c)], axis=-1).T
    return cos_tt, sin_tt


def _trunk(x, params, tables):
    (norm_e, w_in_e, pool_w, pool_scale, q_norm, k_norm, w_out_e,
     norm_o, w_in_o, sgu_norm, w_s, b_s, w_out_o) = params
    for j in range(DEPTH // 2):
        q_g = jnp.broadcast_to(jnp.tile(q_norm[j], N_Q_HEADS)[:, None], (ATTN_WIDTH, ROW_TILE))
        k_g = jnp.broadcast_to(jnp.tile(k_norm[j], N_KV_HEADS)[:, None], (KV_WIDTH, ROW_TILE))
        a, az, qt, k, vx = _even_in(x, norm_e[j][None, :], w_in_e[j], q_g, k_g, tables[0], tables[1],
                                    pool_w[j], pool_scale[j][None, :])
        b = _attention(qt, k, vx, az)
        bs_b = jnp.broadcast_to(b_s[j][:, :, None], (SGU_GROUPS, CHUNK, SGU_GC))
        x = _out_odd(a, b, x, w_out_e[j], norm_o[j][None, :], w_in_o[j], sgu_norm[j][None, :],
                     w_s[j], bs_b, w_out_o[j])
    return x


def kernel(x_prompt, x_sample, norm_e, w_in_e, pool_w, pool_scale, q_norm, k_norm, w_out_e,
           norm_o, w_in_o, sgu_norm, w_s, b_s, w_out_o):
    params = (norm_e, w_in_e.astype(BF16), pool_w.astype(BF16), pool_scale, q_norm, k_norm,
              w_out_e.astype(BF16), norm_o, w_in_o.astype(BF16), sgu_norm, w_s.astype(BF16), b_s,
              w_out_o.astype(BF16))
    tables = _rope_tables(x_prompt.shape[1])
    y_prompt = _trunk(x_prompt, params, tables)
    y_sample = _trunk(x_sample, params, tables)
    return (y_prompt, y_sample)
```

```python
import math

import jax
import jax.numpy as jnp
import numpy as np
from jax import lax
from jax.experimental import pallas as pl
from jax.experimental.pallas import tpu as pltpu

F32 = jnp.float32
BF16 = jnp.bfloat16

D_MODEL = 1024
DEPTH = 4
GRID_W = 64
EPS = 1e-6

POOL_WIDTH = 512
POOL_WINDOWS = (2, 4, 8, 16)
POOL_GC = 128

HEAD_DIM = 64
N_Q_HEADS = 8
N_KV_HEADS = 2
Q_PER_KV = 4
ATTN_WIDTH = 512
KV_WIDTH = 128
ROPE_FREQ = 16
ROPE_THETA = 10000.0

OFF_PU, OFF_PZ, OFF_Q, OFF_K, OFF_V, OFF_AZ, EVEN_IN = 0, 512, 1024, 1536, 1664, 1792, 2304

SGU_WIDTH = 1024
SGU_GROUPS = 8
SGU_GC = 128
CHUNK = 128

LANES = 128
HALO = 8

ROW_TILE = 512
Q_TILE = 256
SUB_TILE = 128
KEY_TILE = 256
ONES_ROWS = 16
Q_PRESCALE = HEAD_DIM ** -0.5 * math.log2(math.e)

VMEM_LIMIT = 56 * 1024 * 1024


def _silu(x):
    return x * (1.0 / (1.0 + jnp.exp(-x)))


def _gelu(x):
    return 0.5 * x * (1.0 + lax.erf(x * np.float32(math.sqrt(0.5))))


def _rms_rows(x, g):
    ms = jnp.mean(x * x, axis=-1, keepdims=True)
    return x * lax.rsqrt(ms + EPS) * g


def _norm_rope_t(t, gain, scale, cos_r, sin_r, cos_c, sin_c):
    ms = jnp.sum(t * t, axis=0, keepdims=True) * np.float32(1.0 / HEAD_DIM)
    y = t * (lax.rsqrt(ms + EPS) * np.float32(scale)) * gain
    x1, x2, x3, x4 = (y[j * ROPE_FREQ:(j + 1) * ROPE_FREQ] for j in range(4))
    return jnp.concatenate([x1 * cos_r - x2 * sin_r, x2 * cos_r + x1 * sin_r,
                            x3 * cos_c - x4 * sin_c, x4 * cos_c + x3 * sin_c], axis=0)


def _even_in_kernel(x_ref, xp_ref, xn_ref, ng_ref, w_ref, qg_ref, kg_ref, cost_ref, sint_ref,
                    pw_ref, ps_ref, a_ref, az_ref, qt_ref, k_ref, vx_ref, h_ref, ext_ref):
    it = pl.program_id(1)
    nt = pl.num_programs(1)
    tm = x_ref.shape[1]
    L = tm * nt
    ng = ng_ref[...]
    h_ref[0:tm] = _rms_rows(x_ref[0], ng).astype(BF16)
    h_ref[tm:tm + 2 * HALO] = _rms_rows(
        jnp.concatenate([xp_ref[0], xn_ref[0]], axis=0), ng).astype(BF16)

    def proj(lo, hi):
        return jnp.dot(h_ref[0:tm], w_ref[:, lo:hi], preferred_element_type=F32)

    u_all = jnp.dot(h_ref[...], w_ref[:, OFF_PU:OFF_PZ], preferred_element_type=F32)
    u = u_all[0:tm]
    ext_ref[0:HALO] = jnp.where(it > 0, u_all[tm:tm + HALO], 0.0)
    ext_ref[HALO:HALO + tm] = u
    ext_ref[HALO + tm:2 * HALO + tm] = jnp.where(it < nt - 1, u_all[tm + HALO:tm + 2 * HALO], 0.0)
    t = it * tm + lax.broadcasted_iota(jnp.int32, (tm, 1), 0)
    parts = []
    for g, w in enumerate(POOL_WINDOWS):
        hw = w // 2
        sl = slice(g * POOL_GC, (g + 1) * POOL_GC)
        acc = None
        for j in range(-hw, hw):
            piece = ext_ref[HALO + j:HALO + j + tm, sl]
            acc = piece if acc is None else acc + piece
        cnt = (jnp.minimum(t + hw, L) - jnp.maximum(t - hw, 0)).astype(F32)
        parts.append((acc / cnt - u[:, sl]).astype(BF16))

    cos_r, cos_c = cost_ref[0:ROPE_FREQ], cost_ref[ROPE_FREQ:2 * ROPE_FREQ]
    sin_r, sin_c = sint_ref[0:ROPE_FREQ], sint_ref[ROPE_FREQ:2 * ROPE_FREQ]
    qt = proj(OFF_Q, OFF_K).T
    for hd in range(N_Q_HEADS):
        rows = slice(hd * HEAD_DIM, (hd + 1) * HEAD_DIM)
        qt_ref[0, rows, :] = _norm_rope_t(qt[rows], qg_ref[rows], Q_PRESCALE,
                                          cos_r, sin_r, cos_c, sin_c).astype(BF16)
    kt = proj(OFF_K, OFF_V).T
    kt = jnp.concatenate(
        [_norm_rope_t(kt[kv * HEAD_DIM:(kv + 1) * HEAD_DIM], kg_ref[kv * HEAD_DIM:(kv + 1) * HEAD_DIM],
                      1.0, cos_r, sin_r, cos_c, sin_c) for kv in range(N_KV_HEADS)], axis=0)
    k_ref[0] = kt.T.astype(BF16)

    vt = proj(OFF_V, OFF_AZ).T
    ones = jnp.ones((ONES_ROWS, tm), BF16)
    for kv in range(N_KV_HEADS):
        vx_ref[0, kv, 0:HEAD_DIM] = vt[kv * HEAD_DIM:(kv + 1) * HEAD_DIM].astype(BF16)
        vx_ref[0, kv, HEAD_DIM:HEAD_DIM + ONES_ROWS] = ones

    gate = _silu(proj(OFF_PZ, OFF_Q))
    mixed = jnp.concatenate([jnp.dot(parts[g], pw_ref[g], preferred_element_type=F32)
                             for g in range(len(POOL_WINDOWS))], axis=1)
    a_ref[0] = (mixed * ps_ref[...] * gate).astype(BF16)
    az_ref[0] = proj(OFF_AZ, EVEN_IN)


def _even_in(x, norm_g, w_in, q_g, k_g, cos_tt, sin_tt, pool_w, pool_scale):
    B, L, _ = x.shape
    tm = ROW_TILE
    nt = L // tm
    hb = tm // HALO
    full = lambda shape: pl.BlockSpec(shape, lambda b, i: (0,) * len(shape))
    tok = lambda w: pl.BlockSpec((1, tm, w), lambda b, i: (b, i, 0))
    vrows = HEAD_DIM + ONES_ROWS
    return pl.pallas_call(
        _even_in_kernel,
        grid=(B, nt),
        in_specs=[
            tok(D_MODEL),
            pl.BlockSpec((1, HALO, D_MODEL), lambda b, i: (b, jnp.maximum(i * hb - 1, 0), 0)),
            pl.BlockSpec((1, HALO, D_MODEL),
                         lambda b, i: (b, jnp.minimum((i + 1) * hb, L // HALO - 1), 0)),
            full((1, D_MODEL)),
            full((D_MODEL, EVEN_IN)),
            full((ATTN_WIDTH, tm)),
            full((KV_WIDTH, tm)),
            pl.BlockSpec((2 * ROPE_FREQ, tm), lambda b, i: (0, i)),
            pl.BlockSpec((2 * ROPE_FREQ, tm), lambda b, i: (0, i)),
            full((len(POOL_WINDOWS), POOL_GC, POOL_GC)),
            full((1, POOL_WIDTH)),
        ],
        out_specs=[tok(POOL_WIDTH), tok(ATTN_WIDTH),
                   pl.BlockSpec((1, ATTN_WIDTH, tm), lambda b, i: (b, 0, i)),
                   tok(KV_WIDTH),
                   pl.BlockSpec((1, N_KV_HEADS, vrows, tm), lambda b, i: (b, 0, 0, i))],
        out_shape=[
            jax.ShapeDtypeStruct((B, L, POOL_WIDTH), BF16),
            jax.ShapeDtypeStruct((B, L, ATTN_WIDTH), F32),
            jax.ShapeDtypeStruct((B, ATTN_WIDTH, L), BF16),
            jax.ShapeDtypeStruct((B, L, KV_WIDTH), BF16),
            jax.ShapeDtypeStruct((B, N_KV_HEADS, vrows, L), BF16),
        ],
        scratch_shapes=[pltpu.VMEM((tm + 2 * HALO, D_MODEL), BF16),
                        pltpu.VMEM((tm + 2 * HALO, POOL_WIDTH), F32)],
        compiler_params=pltpu.CompilerParams(
            dimension_semantics=("arbitrary", "arbitrary"), vmem_limit_bytes=VMEM_LIMIT),
        name="even_in",
    )(x, x, x, norm_g, w_in, q_g, k_g, cos_tt, sin_tt, pool_w, pool_scale)


def _attn_kernel(qt_ref, k_ref, vx_ref, az_ref, o_ref, s_ref, m_ref):
    i = pl.program_id(0)
    n_tiles = pl.num_programs(0) - 1
    L = k_ref.shape[1]
    nq = L // Q_TILE
    nc = L // KEY_TILE
    R = Q_PER_KV * SUB_TILE

    @pl.when(i == 0)
    def _():
        s_ref[...] = jnp.zeros(s_ref.shape, F32)
        m_ref[...] = jnp.zeros(m_ref.shape, F32)

    kvh = jnp.right_shift(jnp.minimum(i, n_tiles - 1), nq.bit_length() - 1) & 1
    for sub in range(Q_TILE // SUB_TILE):
        tok = slice(sub * SUB_TILE, (sub + 1) * SUB_TILE)
        qt = qt_ref[0, :, tok]
        qcat = jnp.concatenate(
            [qt[g * HEAD_DIM:(g + 1) * HEAD_DIM, :] for g in range(Q_PER_KV)], axis=1)
        zero = jnp.zeros_like(qcat)
        qx = jnp.concatenate([jnp.where(kvh == 0, qcat, zero),
                              jnp.where(kvh == 1, qcat, zero)], axis=0)
        m_prev = jnp.max(m_ref[sub], axis=0, keepdims=True)
        m_new = None
        acc = None
        for c in range(nc):
            keys = slice(c * KEY_TILE, (c + 1) * KEY_TILE)
            s_c = jnp.dot(k_ref[0, keys, :], qx, preferred_element_type=F32)
            p = jnp.exp2(s_ref[sub, c] - m_prev).astype(BF16)
            d = jnp.dot(vx_ref[0, 0, :, keys], p, preferred_element_type=F32)
            acc = d if acc is None else acc + d
            s_ref[sub, c] = s_c
            mc = jnp.max(s_c.reshape(KEY_TILE // HALO, HALO, R), axis=0)
            m_new = mc if m_new is None else jnp.maximum(m_new, mc)
        m_ref[sub] = m_new
        out = acc[0:HEAD_DIM] * (1.0 / acc[HEAD_DIM:HEAD_DIM + 1])
        gate = _silu(az_ref[0, tok, :])
        for pair in range(Q_PER_KV // 2):
            two = jnp.concatenate(
                [out[:, (2 * pair) * SUB_TILE:(2 * pair + 1) * SUB_TILE],
                 out[:, (2 * pair + 1) * SUB_TILE:(2 * pair + 2) * SUB_TILE]], axis=0)
            lanes = slice(pair * LANES, (pair + 1) * LANES)
            o_ref[0, tok, lanes] = (two.T * gate[:, lanes]).astype(BF16)


def _attention(qt, k, vx, az):
    B, _, L = qt.shape
    tq = Q_TILE
    nq = L // tq
    gw = Q_PER_KV * HEAD_DIM
    n_tiles = B * N_KV_HEADS * nq
    nq_bits = nq.bit_length() - 1
    assert nq == 1 << nq_bits and N_KV_HEADS == 2

    def tile(t):
        return jnp.right_shift(t, nq_bits + 1), jnp.right_shift(t, nq_bits) & 1, t & (nq - 1)

    def cur(i):
        return tile(jnp.minimum(i, n_tiles - 1))

    def prev(i):
        return tile(jnp.maximum(i - 1, 0))

    def q_map(i):
        b, h, iq = cur(i)
        return (b, h, iq)

    def k_map(i):
        return (cur(i)[0], 0, 0)

    def v_map(i):
        b, h, _ = prev(i)
        return (b, h, 0, 0)

    def o_map(i):
        b, h, iq = prev(i)
        return (b, iq, h)

    R = Q_PER_KV * SUB_TILE
    nsub = tq // SUB_TILE
    return pl.pallas_call(
        _attn_kernel,
        grid=(n_tiles + 1,),
        in_specs=[
            pl.BlockSpec((1, gw, tq), q_map),
            pl.BlockSpec((1, L, KV_WIDTH), k_map),
            pl.BlockSpec((1, 1, HEAD_DIM + ONES_ROWS, L), v_map),
            pl.BlockSpec((1, tq, gw), o_map),
        ],
        out_specs=pl.BlockSpec((1, tq, gw), o_map),
        out_shape=jax.ShapeDtypeStruct((B, L, ATTN_WIDTH), BF16),
        scratch_shapes=[pltpu.VMEM((nsub, L // KEY_TILE, KEY_TILE, R), F32),
                        pltpu.VMEM((nsub, HALO, R), F32)],
        compiler_params=pltpu.CompilerParams(
            dimension_semantics=("arbitrary",), vmem_limit_bytes=VMEM_LIMIT),
        name="attention",
    )(qt, k, vx, az)


def _out_odd_kernel(a_ref, b_ref, x_ref, woe_ref, ng_ref, wi_ref, sg_ref, ws_ref, bs_ref, wo_ref,
                    out_ref, sv_ref):
    tm = x_ref.shape[1]
    cat = jnp.concatenate([a_ref[0], b_ref[0]], axis=1)
    x = x_ref[0] + jnp.dot(cat, woe_ref[...], preferred_element_type=F32)
    h = _rms_rows(x, ng_ref[...]).astype(BF16)

    def proj(j):
        return jnp.dot(h, wi_ref[:, j * SGU_WIDTH:(j + 1) * SGU_WIDTH], preferred_element_type=F32)

    vv = _rms_rows(_gelu(proj(1)), sg_ref[...]).astype(BF16)
    nch = tm // CHUNK
    for g in range(SGU_GROUPS):
        gs = slice(g * SGU_GC, (g + 1) * SGU_GC)
        vcat = jnp.concatenate([vv[n * CHUNK:(n + 1) * CHUNK, gs] for n in range(nch)], axis=1)
        res = jnp.dot(ws_ref[g], vcat, preferred_element_type=F32)
        for n in range(nch):
            sv_ref[n * CHUNK:(n + 1) * CHUNK, gs] = res[:, n * SGU_GC:(n + 1) * SGU_GC] + bs_ref[g]
    y = (_gelu(proj(0)) * sv_ref[...] * _silu(proj(2))).astype(BF16)
    out_ref[0] = x + jnp.dot(y, wo_ref[...], preferred_element_type=F32)


def _out_odd(a, b, x, w_out_e, norm_g, w_in, sgu_g, w_s, b_s, w_out):
    B, L, _ = x.shape
    tm = ROW_TILE
    full = lambda shape: pl.BlockSpec(shape, lambda bb, i: (0,) * len(shape))
    tok = lambda w: pl.BlockSpec((1, tm, w), lambda bb, i: (bb, i, 0))
    return pl.pallas_call(
        _out_odd_kernel,
        grid=(B, L // tm),
        in_specs=[
            tok(POOL_WIDTH), tok(ATTN_WIDTH), tok(D_MODEL),
            full((D_MODEL, D_MODEL)),
            full((1, D_MODEL)),
            full((D_MODEL, 3 * SGU_WIDTH)),
            full((1, SGU_WIDTH)),
            full((SGU_GROUPS, CHUNK, CHUNK)),
            full((SGU_GROUPS, CHUNK, SGU_GC)),
            full((SGU_WIDTH, D_MODEL)),
        ],
        out_specs=tok(D_MODEL),
        out_shape=jax.ShapeDtypeStruct((B, L, D_MODEL), F32),
        scratch_shapes=[pltpu.VMEM((tm, SGU_WIDTH), F32)],
        compiler_params=pltpu.CompilerParams(
            dimension_semantics=("arbitrary", "arbitrary"), vmem_limit_bytes=VMEM_LIMIT),
        name="out_odd",
    )(a, b, x, w_out_e, norm_g, w_in, sgu_g, w_s, b_s, w_out)


def _rope_tables(L):
    rows_n = L // GRID_W
    row = jnp.repeat(jnp.arange(rows_n), GRID_W).astype(F32)
    col = jnp.tile(jnp.arange(GRID_W), rows_n).astype(F32)
    inv = 1.0 / (ROPE_THETA ** (jnp.arange(ROPE_FREQ, dtype=F32) / ROPE_FREQ))
    ang_r = row[:, None] * inv[None, :]
    ang_c = col[:, None] * inv[None, :]
    cos_tt = jnp.concatenate([jnp.cos(ang_r), jnp.cos(ang_c)], axis=-1).T
    sin_tt = jnp.concatenate([jnp.sin(ang_r), jnp.sin(ang_c)], axis=-1).T
    return cos_tt, sin_tt


def _trunk(x, params, tables):
    (norm_e, w_in_e, pool_w, pool_scale, q_norm, k_norm, w_out_e,
     norm_o, w_in_o, sgu_norm, w_s, b_s, w_out_o) = params
    for j in range(DEPTH // 2):
        q_g = jnp.broadcast_to(jnp.tile(q_norm[j], N_Q_HEADS)[:, None], (ATTN_WIDTH, ROW_TILE))
        k_g = jnp.broadcast_to(jnp.tile(k_norm[j], N_KV_HEADS)[:, None], (KV_WIDTH, ROW_TILE))
        a, az, qt, k, vx = _even_in(x, norm_e[j][None, :], w_in_e[j], q_g, k_g, tables[0], tables[1],
                                    pool_w[j], pool_scale[j][None, :])
        b = _attention(qt, k, vx, az)
        bs_b = jnp.broadcast_to(b_s[j][:, :, None], (SGU_GROUPS, CHUNK, SGU_GC))
        x = _out_odd(a, b, x, w_out_e[j], norm_o[j][None, :], w_in_o[j], sgu_norm[j][None, :],
                     w_s[j], bs_b, w_out_o[j])
    return x


def kernel(x_prompt, x_sample, norm_e, w_in_e, pool_w, pool_scale, q_norm, k_norm, w_out_e,
           norm_o, w_in_o, sgu_norm, w_s, b_s, w_out_o):
    params = (norm_e, w_in_e.astype(BF16), pool_w.astype(BF16), pool_scale, q_norm, k_norm,
              w_out_e.astype(BF16), norm_o, w_in_o.astype(BF16), sgu_norm, w_s.astype(BF16), b_s,
              w_out_o.astype(BF16))
    tables = _rope_tables(x_prompt.shape[1])
    y_prompt = _trunk(x_prompt, params, tables)
    y_sample = _trunk(x_sample, params, tables)
    return (y_prompt, y_sample)
```

```python
import math

import jax
import jax.numpy as jnp
import numpy as np
from jax import lax
from jax.experimental import pallas as pl
from jax.experimental.pallas import tpu as pltpu

F32 = jnp.float32
BF16 = jnp.bfloat16

D_MODEL = 1024
DEPTH = 4
GRID_W = 64
EPS = 1e-6

POOL_WIDTH = 512
POOL_WINDOWS = (2, 4, 8, 16)
POOL_GC = 128

HEAD_DIM = 64
N_Q_HEADS = 8
N_KV_HEADS = 2
Q_PER_KV = 4
ATTN_WIDTH = 512
KV_WIDTH = 128
ROPE_FREQ = 16
ROPE_THETA = 10000.0

OFF_PU, OFF_PZ, OFF_Q, OFF_K, OFF_V, OFF_AZ, EVEN_IN = 0, 512, 1024, 1536, 1664, 1792, 2304

SGU_WIDTH = 1024
SGU_GROUPS = 8
SGU_GC = 128
CHUNK = 128

LANES = 128
HALO = 8

ROW_TILE = 512
Q_TILE = 512
SUB_TILE = 128
KEY_TILE = 256
ONES_ROWS = 16
Q_PRESCALE = HEAD_DIM ** -0.5 * math.log2(math.e)

VMEM_LIMIT = 56 * 1024 * 1024


def _silu(x):
    return x * (1.0 / (1.0 + jnp.exp(-x)))


def _gelu(x):
    return 0.5 * x * (1.0 + lax.erf(x * np.float32(math.sqrt(0.5))))


def _rms_rows(x, g):
    ms = jnp.mean(x * x, axis=-1, keepdims=True)
    return x * lax.rsqrt(ms + EPS) * g


def _norm_rope_t(t, gain, scale, cos_r, sin_r, cos_c, sin_c):
    ms = jnp.sum(t * t, axis=0, keepdims=True) * np.float32(1.0 / HEAD_DIM)
    y = t * (lax.rsqrt(ms + EPS) * np.float32(scale)) * gain
    x1, x2, x3, x4 = (y[j * ROPE_FREQ:(j + 1) * ROPE_FREQ] for j in range(4))
    return jnp.concatenate([x1 * cos_r - x2 * sin_r, x2 * cos_r + x1 * sin_r,
                            x3 * cos_c - x4 * sin_c, x4 * cos_c + x3 * sin_c], axis=0)


def _even_in_kernel(x_ref, xp_ref, xn_ref, ng_ref, w_ref, qg_ref, kg_ref, cost_ref, sint_ref,
                    pw_ref, ps_ref, a_ref, az_ref, qt_ref, k_ref, vx_ref, h_ref, ext_ref):
    it = pl.program_id(1)
    nt = pl.num_programs(1)
    tm = x_ref.shape[1]
    L = tm * nt
    ng = ng_ref[...]
    h_ref[0:tm] = _rms_rows(x_ref[0], ng).astype(BF16)
    h_ref[tm:tm + 2 * HALO] = _rms_rows(
        jnp.concatenate([xp_ref[0], xn_ref[0]], axis=0), ng).astype(BF16)

    def proj(lo, hi):
        return jnp.dot(h_ref[0:tm], w_ref[:, lo:hi], preferred_element_type=F32)

    u_all = jnp.dot(h_ref[...], w_ref[:, OFF_PU:OFF_PZ], preferred_element_type=F32)
    u = u_all[0:tm]
    ext_ref[0:HALO] = jnp.where(it > 0, u_all[tm:tm + HALO], 0.0)
    ext_ref[HALO:HALO + tm] = u
    ext_ref[HALO + tm:2 * HALO + tm] = jnp.where(it < nt - 1, u_all[tm + HALO:tm + 2 * HALO], 0.0)
    t = it * tm + lax.broadcasted_iota(jnp.int32, (tm, 1), 0)
    parts = []
    for g, w in enumerate(POOL_WINDOWS):
        hw = w // 2
        sl = slice(g * POOL_GC, (g + 1) * POOL_GC)
        acc = None
        for j in range(-hw, hw):
            piece = ext_ref[HALO + j:HALO + j + tm, sl]
            acc = piece if acc is None else acc + piece
        cnt = (jnp.minimum(t + hw, L) - jnp.maximum(t - hw, 0)).astype(F32)
        parts.append((acc / cnt - u[:, sl]).astype(BF16))

    cos_r, cos_c = cost_ref[0:ROPE_FREQ], cost_ref[ROPE_FREQ:2 * ROPE_FREQ]
    sin_r, sin_c = sint_ref[0:ROPE_FREQ], sint_ref[ROPE_FREQ:2 * ROPE_FREQ]
    qt = proj(OFF_Q, OFF_K).T
    for hd in range(N_Q_HEADS):
        rows = slice(hd * HEAD_DIM, (hd + 1) * HEAD_DIM)
        qt_ref[0, rows, :] = _norm_rope_t(qt[rows], qg_ref[rows], Q_PRESCALE,
                                          cos_r, sin_r, cos_c, sin_c).astype(BF16)
    kt = proj(OFF_K, OFF_V).T
    kt = jnp.concatenate(
        [_norm_rope_t(kt[kv * HEAD_DIM:(kv + 1) * HEAD_DIM], kg_ref[kv * HEAD_DIM:(kv + 1) * HEAD_DIM],
                      1.0, cos_r, sin_r, cos_c, sin_c) for kv in range(N_KV_HEADS)], axis=0)
    k_ref[0] = kt.T.astype(BF16)

    vt = proj(OFF_V, OFF_AZ).T
    ones = jnp.ones((ONES_ROWS, tm), BF16)
    for kv in range(N_KV_HEADS):
        vx_ref[0, kv, 0:HEAD_DIM] = vt[kv * HEAD_DIM:(kv + 1) * HEAD_DIM].astype(BF16)
        vx_ref[0, kv, HEAD_DIM:HEAD_DIM + ONES_ROWS] = ones

    gate = _silu(proj(OFF_PZ, OFF_Q))
    mixed = jnp.concatenate([jnp.dot(parts[g], pw_ref[g], preferred_element_type=F32)
                             for g in range(len(POOL_WINDOWS))], axis=1)
    a_ref[0] = (mixed * ps_ref[...] * gate).astype(BF16)
    az_ref[0] = proj(OFF_AZ, EVEN_IN)


def _even_in(x, norm_g, w_in, q_g, k_g, cos_tt, sin_tt, pool_w, pool_scale):
    B, L, _ = x.shape
    tm = ROW_TILE
    nt = L // tm
    hb = tm // HALO
    full = lambda shape: pl.BlockSpec(shape, lambda b, i: (0,) * len(shape))
    tok = lambda w: pl.BlockSpec((1, tm, w), lambda b, i: (b, i, 0))
    vrows = HEAD_DIM + ONES_ROWS
    return pl.pallas_call(
        _even_in_kernel,
        grid=(B, nt),
        in_specs=[
            tok(D_MODEL),
            pl.BlockSpec((1, HALO, D_MODEL), lambda b, i: (b, jnp.maximum(i * hb - 1, 0), 0)),
            pl.BlockSpec((1, HALO, D_MODEL),
                         lambda b, i: (b, jnp.minimum((i + 1) * hb, L // HALO - 1), 0)),
            full((1, D_MODEL)),
            full((D_MODEL, EVEN_IN)),
            full((ATTN_WIDTH, tm)),
            full((KV_WIDTH, tm)),
            pl.BlockSpec((2 * ROPE_FREQ, tm), lambda b, i: (0, i)),
            pl.BlockSpec((2 * ROPE_FREQ, tm), lambda b, i: (0, i)),
            full((len(POOL_WINDOWS), POOL_GC, POOL_GC)),
            full((1, POOL_WIDTH)),
        ],
        out_specs=[tok(POOL_WIDTH), tok(ATTN_WIDTH),
                   pl.BlockSpec((1, ATTN_WIDTH, tm), lambda b, i: (b, 0, i)),
                   tok(KV_WIDTH),
                   pl.BlockSpec((1, N_KV_HEADS, vrows, tm), lambda b, i: (b, 0, 0, i))],
        out_shape=[
            jax.ShapeDtypeStruct((B, L, POOL_WIDTH), BF16),
            jax.ShapeDtypeStruct((B, L, ATTN_WIDTH), F32),
            jax.ShapeDtypeStruct((B, ATTN_WIDTH, L), BF16),
            jax.ShapeDtypeStruct((B, L, KV_WIDTH), BF16),
            jax.ShapeDtypeStruct((B, N_KV_HEADS, vrows, L), BF16),
        ],
        scratch_shapes=[pltpu.VMEM((tm + 2 * HALO, D_MODEL), BF16),
                        pltpu.VMEM((tm + 2 * HALO, POOL_WIDTH), F32)],
        compiler_params=pltpu.CompilerParams(
            dimension_semantics=("arbitrary", "arbitrary"), vmem_limit_bytes=VMEM_LIMIT),
        name="even_in",
    )(x, x, x, norm_g, w_in, q_g, k_g, cos_tt, sin_tt, pool_w, pool_scale)


def _attn_kernel(qt_ref, k_ref, vx_ref, az_ref, o_ref, s_ref, m_ref):
    i = pl.program_id(0)
    n_tiles = pl.num_programs(0) - 1
    L = k_ref.shape[1]
    nq = L // Q_TILE
    nc = L // KEY_TILE
    R = Q_PER_KV * SUB_TILE

    @pl.when(i == 0)
    def _():
        s_ref[...] = jnp.zeros(s_ref.shape, F32)
        m_ref[...] = jnp.zeros(m_ref.shape, F32)

    kvh = jnp.right_shift(jnp.minimum(i, n_tiles - 1), nq.bit_length() - 1) & 1
    for sub in range(Q_TILE // SUB_TILE):
        tok = slice(sub * SUB_TILE, (sub + 1) * SUB_TILE)
        qt = qt_ref[0, :, tok]
        qcat = jnp.concatenate(
            [qt[g * HEAD_DIM:(g + 1) * HEAD_DIM, :] for g in range(Q_PER_KV)], axis=1)
        zero = jnp.zeros_like(qcat)
        qx = jnp.concatenate([jnp.where(kvh == 0, qcat, zero),
                              jnp.where(kvh == 1, qcat, zero)], axis=0)
        m_prev = jnp.max(m_ref[sub], axis=0, keepdims=True)
        m_new = None
        acc = None
        for c in range(nc):
            keys = slice(c * KEY_TILE, (c + 1) * KEY_TILE)
            s_c = jnp.dot(k_ref[0, keys, :], qx, preferred_element_type=F32)
            p = jnp.exp2(s_ref[sub, c] - m_prev).astype(BF16)
            d = jnp.dot(vx_ref[0, 0, :, keys], p, preferred_element_type=F32)
            acc = d if acc is None else acc + d
            s_ref[sub, c] = s_c
            mc = jnp.max(s_c.reshape(KEY_TILE // HALO, HALO, R), axis=0)
            m_new = mc if m_new is None else jnp.maximum(m_new, mc)
        m_ref[sub] = m_new
        out = acc[0:HEAD_DIM] * (1.0 / acc[HEAD_DIM:HEAD_DIM + 1])
        gate = _silu(az_ref[0, tok, :])
        for pair in range(Q_PER_KV // 2):
            two = jnp.concatenate(
                [out[:, (2 * pair) * SUB_TILE:(2 * pair + 1) * SUB_TILE],
                 out[:, (2 * pair + 1) * SUB_TILE:(2 * pair + 2) * SUB_TILE]], axis=0)
            lanes = slice(pair * LANES, (pair + 1) * LANES)
            o_ref[0, tok, lanes] = (two.T * gate[:, lanes]).astype(BF16)


def _attention(qt, k, vx, az):
    B, _, L = qt.shape
    tq = Q_TILE
    nq = L // tq
    gw = Q_PER_KV * HEAD_DIM
    n_tiles = B * N_KV_HEADS * nq
    nq_bits = nq.bit_length() - 1
    assert nq == 1 << nq_bits and N_KV_HEADS == 2

    def tile(t):
        return jnp.right_shift(t, nq_bits + 1), jnp.right_shift(t, nq_bits) & 1, t & (nq - 1)

    def cur(i):
        return tile(jnp.minimum(i, n_tiles - 1))

    def prev(i):
        return tile(jnp.maximum(i - 1, 0))

    def q_map(i):
        b, h, iq = cur(i)
        return (b, h, iq)

    def k_map(i):
        return (cur(i)[0], 0, 0)

    def v_map(i):
        b, h, _ = prev(i)
        return (b, h, 0, 0)

    def o_map(i):
        b, h, iq = prev(i)
        return (b, iq, h)

    R = Q_PER_KV * SUB_TILE
    nsub = tq // SUB_TILE
    return pl.pallas_call(
        _attn_kernel,
        grid=(n_tiles + 1,),
        in_specs=[
            pl.BlockSpec((1, gw, tq), q_map),
            pl.BlockSpec((1, L, KV_WIDTH), k_map),
            pl.BlockSpec((1, 1, HEAD_DIM + ONES_ROWS, L), v_map),
            pl.BlockSpec((1, tq, gw), o_map),
        ],
        out_specs=pl.BlockSpec((1, tq, gw), o_map),
        out_shape=jax.ShapeDtypeStruct((B, L, ATTN_WIDTH), BF16),
        scratch_shapes=[pltpu.VMEM((nsub, L // KEY_TILE, KEY_TILE, R), F32),
                        pltpu.VMEM((nsub, HALO, R), F32)],
        compiler_params=pltpu.CompilerParams(
            dimension_semantics=("arbitrary",), vmem_limit_bytes=VMEM_LIMIT),
        name="attention",
    )(qt, k, vx, az)


def _out_odd_kernel(a_ref, b_ref, x_ref, woe_ref, ng_ref, wi_ref, sg_ref, ws_ref, bs_ref, wo_ref,
                    out_ref, sv_ref):
    tm = x_ref.shape[1]
    cat = jnp.concatenate([a_ref[0], b_ref[0]], axis=1)
    x = x_ref[0] + jnp.dot(cat, woe_ref[...], preferred_element_type=F32)
    h = _rms_rows(x, ng_ref[...]).astype(BF16)

    def proj(j):
        return jnp.dot(h, wi_ref[:, j * SGU_WIDTH:(j + 1) * SGU_WIDTH], preferred_element_type=F32)

    vv = _rms_rows(_gelu(proj(1)), sg_ref[...]).astype(BF16)
    nch = tm // CHUNK
    for g in range(SGU_GROUPS):
        gs = slice(g * SGU_GC, (g + 1) * SGU_GC)
        vcat = jnp.concatenate([vv[n * CHUNK:(n + 1) * CHUNK, gs] for n in range(nch)], axis=1)
        res = jnp.dot(ws_ref[g], vcat, preferred_element_type=F32)
        for n in range(nch):
            sv_ref[n * CHUNK:(n + 1) * CHUNK, gs] = res[:, n * SGU_GC:(n + 1) * SGU_GC] + bs_ref[g]
    y = (_gelu(proj(0)) * sv_ref[...] * _silu(proj(2))).astype(BF16)
    out_ref[0] = x + jnp.dot(y, wo_ref[...], preferred_element_type=F32)


def _out_odd(a, b, x, w_out_e, norm_g, w_in, sgu_g, w_s, b_s, w_out):
    B, L, _ = x.shape
    tm = ROW_TILE
    full = lambda shape: pl.BlockSpec(shape, lambda bb, i: (0,) * len(shape))
    tok = lambda w: pl.BlockSpec((1, tm, w), lambda bb, i: (bb, i, 0))
    return pl.pallas_call(
        _out_odd_kernel,
        grid=(B, L // tm),
        in_specs=[
            tok(POOL_WIDTH), tok(ATTN_WIDTH), tok(D_MODEL),
            full((D_MODEL, D_MODEL)),
            full((1, D_MODEL)),
            full((D_MODEL, 3 * SGU_WIDTH)),
            full((1, SGU_WIDTH)),
            full((SGU_GROUPS, CHUNK, CHUNK)),
            full((SGU_GROUPS, CHUNK, SGU_GC)),
            full((SGU_WIDTH, D_MODEL)),
        ],
        out_specs=tok(D_MODEL),
        out_shape=jax.ShapeDtypeStruct((B, L, D_MODEL), F32),
        scratch_shapes=[pltpu.VMEM((tm, SGU_WIDTH), F32)],
        compiler_params=pltpu.CompilerParams(
            dimension_semantics=("arbitrary", "arbitrary"), vmem_limit_bytes=VMEM_LIMIT),
        name="out_odd",
    )(a, b, x, w_out_e, norm_g, w_in, sgu_g, w_s, b_s, w_out)


def _rope_tables(L):
    rows_n = L // GRID_W
    row = jnp.repeat(jnp.arange(rows_n), GRID_W).astype(F32)
    col = jnp.tile(jnp.arange(GRID_W), rows_n).astype(F32)
    inv = 1.0 / (ROPE_THETA ** (jnp.arange(ROPE_FREQ, dtype=F32) / ROPE_FREQ))
    ang_r = row[:, None] * inv[None, :]
    ang_c = col[:, None] * inv[None, :]
    cos_tt = jnp.concatenate([jnp.cos(ang_r), jnp.cos(ang_c)], axis=-1).T
    sin_tt = jnp.concatenate([jnp.sin(ang_r), jnp.sin(ang_c)], axis=-1).T
    return cos_tt, sin_tt


def _trunk(x, params, tables):
    (norm_e, w_in_e, pool_w, pool_scale, q_norm, k_norm, w_out_e,
     norm_o, w_in_o, sgu_norm, w_s, b_s, w_out_o) = params
    for j in range(DEPTH // 2):
        q_g = jnp.broadcast_to(jnp.tile(q_norm[j], N_Q_HEADS)[:, None], (ATTN_WIDTH, ROW_TILE))
        k_g = jnp.broadcast_to(jnp.tile(k_norm[j], N_KV_HEADS)[:, None], (KV_WIDTH, ROW_TILE))
        a, az, qt, k, vx = _even_in(x, norm_e[j][None, :], w_in_e[j], q_g, k_g, tables[0], tables[1],
                                    pool_w[j], pool_scale[j][None, :])
        b = _attention(qt, k, vx, az)
        bs_b = jnp.broadcast_to(b_s[j][:, :, None], (SGU_GROUPS, CHUNK, SGU_GC))
        x = _out_odd(a, b, x, w_out_e[j], norm_o[j][None, :], w_in_o[j], sgu_norm[j][None, :],
                     w_s[j], bs_b, w_out_o[j])
    return x


def kernel(x_prompt, x_sample, norm_e, w_in_e, pool_w, pool_scale, q_norm, k_norm, w_out_e,
           norm_o, w_in_o, sgu_norm, w_s, b_s, w_out_o):
    params = (norm_e, w_in_e.astype(BF16), pool_w.astype(BF16), pool_scale, q_norm, k_norm,
              w_out_e.astype(BF16), norm_o, w_in_o.astype(BF16), sgu_norm, w_s.astype(BF16), b_s,
              w_out_o.astype(BF16))
    tables = _rope_tables(x_prompt.shape[1])
    y_prompt = _trunk(x_prompt, params, tables)
    y_sample = _trunk(x_sample, params, tables)
    return (y_prompt, y_sample)
```

```python
import math

import jax
import jax.numpy as jnp
import numpy as np
from jax import lax
from jax.experimental import pallas as pl
from jax.experimental.pallas import tpu as pltpu

F32 = jnp.float32
BF16 = jnp.bfloat16

D_MODEL = 1024
DEPTH = 4
GRID_W = 64
EPS = 1e-6

POOL_WIDTH = 512
POOL_WINDOWS = (2, 4, 8, 16)
POOL_GC = 128

HEAD_DIM = 64
N_Q_HEADS = 8
N_KV_HEADS = 2
Q_PER_KV = 4
ATTN_WIDTH = 512
KV_WIDTH = 128
ROPE_FREQ = 16
ROPE_THETA = 10000.0

OFF_PU, OFF_PZ, OFF_Q, OFF_K, OFF_V, OFF_AZ, EVEN_IN = 0, 512, 1024, 1536, 1664, 1792, 2304

SGU_WIDTH = 1024
SGU_GROUPS = 8
SGU_GC = 128
CHUNK = 128

LANES = 128
HALO = 8

ROW_TILE = 512
IN_ROW_TILE = 1024
ROW_SUB = 256
Q_TILE = 512
SUB_TILE = 128
KEY_TILE = 256
ONES_ROWS = 16
Q_PRESCALE = HEAD_DIM ** -0.5 * math.log2(math.e)

VMEM_LIMIT = 56 * 1024 * 1024


def _silu(x):
    return x * (1.0 / (1.0 + jnp.exp(-x)))


def _gelu(x):
    return 0.5 * x * (1.0 + lax.erf(x * np.float32(math.sqrt(0.5))))


def _rms_rows(x, g):
    ms = jnp.mean(x * x, axis=-1, keepdims=True)
    return x * lax.rsqrt(ms + EPS) * g


def _norm_rope_t(t, gain, scale, cos_r, sin_r, cos_c, sin_c):
    ms = jnp.sum(t * t, axis=0, keepdims=True) * np.float32(1.0 / HEAD_DIM)
    y = t * (lax.rsqrt(ms + EPS) * np.float32(scale)) * gain
    x1, x2, x3, x4 = (y[j * ROPE_FREQ:(j + 1) * ROPE_FREQ] for j in range(4))
    return jnp.concatenate([x1 * cos_r - x2 * sin_r, x2 * cos_r + x1 * sin_r,
                            x3 * cos_c - x4 * sin_c, x4 * cos_c + x3 * sin_c], axis=0)


def _even_in_kernel(x_ref, xp_ref, xn_ref, ng_ref, w_ref, qg_ref, kg_ref, cost_ref, sint_ref,
                    pw_ref, ps_ref, a_ref, az_ref, qt_ref, k_ref, vx_ref, h_ref, ext_ref):
    it = pl.program_id(1)
    nt = pl.num_programs(1)
    tm = x_ref.shape[1]
    L = tm * nt
    ng = ng_ref[...]
    h_ref[0:tm] = _rms_rows(x_ref[0], ng).astype(BF16)
    h_ref[tm:tm + 2 * HALO] = _rms_rows(
        jnp.concatenate([xp_ref[0], xn_ref[0]], axis=0), ng).astype(BF16)

    def proj(lo, hi):
        return jnp.dot(h_ref[0:tm], w_ref[:, lo:hi], preferred_element_type=F32)

    u_all = jnp.dot(h_ref[...], w_ref[:, OFF_PU:OFF_PZ], preferred_element_type=F32)
    u = u_all[0:tm]
    ext_ref[0:HALO] = jnp.where(it > 0, u_all[tm:tm + HALO], 0.0)
    ext_ref[HALO:HALO + tm] = u
    ext_ref[HALO + tm:2 * HALO + tm] = jnp.where(it < nt - 1, u_all[tm + HALO:tm + 2 * HALO], 0.0)
    t = it * tm + lax.broadcasted_iota(jnp.int32, (tm, 1), 0)
    parts = []
    for g, w in enumerate(POOL_WINDOWS):
        hw = w // 2
        sl = slice(g * POOL_GC, (g + 1) * POOL_GC)
        acc = None
        for j in range(-hw, hw):
            piece = ext_ref[HALO + j:HALO + j + tm, sl]
            acc = piece if acc is None else acc + piece
        cnt = (jnp.minimum(t + hw, L) - jnp.maximum(t - hw, 0)).astype(F32)
        parts.append((acc / cnt - u[:, sl]).astype(BF16))

    cos_r, cos_c = cost_ref[0:ROPE_FREQ], cost_ref[ROPE_FREQ:2 * ROPE_FREQ]
    sin_r, sin_c = sint_ref[0:ROPE_FREQ], sint_ref[ROPE_FREQ:2 * ROPE_FREQ]
    qt = proj(OFF_Q, OFF_K).T
    for hd in range(N_Q_HEADS):
        rows = slice(hd * HEAD_DIM, (hd + 1) * HEAD_DIM)
        qt_ref[0, rows, :] = _norm_rope_t(qt[rows], qg_ref[rows], Q_PRESCALE,
                                          cos_r, sin_r, cos_c, sin_c).astype(BF16)
    kt = proj(OFF_K, OFF_V).T
    kt = jnp.concatenate(
        [_norm_rope_t(kt[kv * HEAD_DIM:(kv + 1) * HEAD_DIM], kg_ref[kv * HEAD_DIM:(kv + 1) * HEAD_DIM],
                      1.0, cos_r, sin_r, cos_c, sin_c) for kv in range(N_KV_HEADS)], axis=0)
    k_ref[0] = kt.T.astype(BF16)

    vt = proj(OFF_V, OFF_AZ).T
    ones = jnp.ones((ONES_ROWS, tm), BF16)
    for kv in range(N_KV_HEADS):
        vx_ref[0, kv, 0:HEAD_DIM] = vt[kv * HEAD_DIM:(kv + 1) * HEAD_DIM].astype(BF16)
        vx_ref[0, kv, HEAD_DIM:HEAD_DIM + ONES_ROWS] = ones

    gate = _silu(proj(OFF_PZ, OFF_Q))
    mixed = jnp.concatenate([jnp.dot(parts[g], pw_ref[g], preferred_element_type=F32)
                             for g in range(len(POOL_WINDOWS))], axis=1)
    a_ref[0] = (mixed * ps_ref[...] * gate).astype(BF16)
    az_ref[0] = proj(OFF_AZ, EVEN_IN)


def _even_in(x, norm_g, w_in, q_g, k_g, cos_tt, sin_tt, pool_w, pool_scale):
    B, L, _ = x.shape
    tm = IN_ROW_TILE
    nt = L // tm
    hb = tm // HALO
    full = lambda shape: pl.BlockSpec(shape, lambda b, i: (0,) * len(shape))
    tok = lambda w: pl.BlockSpec((1, tm, w), lambda b, i: (b, i, 0))
    vrows = HEAD_DIM + ONES_ROWS
    return pl.pallas_call(
        _even_in_kernel,
        grid=(B, nt),
        in_specs=[
            tok(D_MODEL),
            pl.BlockSpec((1, HALO, D_MODEL), lambda b, i: (b, jnp.maximum(i * hb - 1, 0), 0)),
            pl.BlockSpec((1, HALO, D_MODEL),
                         lambda b, i: (b, jnp.minimum((i + 1) * hb, L // HALO - 1), 0)),
            full((1, D_MODEL)),
            full((D_MODEL, EVEN_IN)),
            full((ATTN_WIDTH, tm)),
            full((KV_WIDTH, tm)),
            pl.BlockSpec((2 * ROPE_FREQ, tm), lambda b, i: (0, i)),
            pl.BlockSpec((2 * ROPE_FREQ, tm), lambda b, i: (0, i)),
            full((len(POOL_WINDOWS), POOL_GC, POOL_GC)),
            full((1, POOL_WIDTH)),
        ],
        out_specs=[tok(POOL_WIDTH), tok(ATTN_WIDTH),
                   pl.BlockSpec((1, ATTN_WIDTH, tm), lambda b, i: (b, 0, i)),
                   tok(KV_WIDTH),
                   pl.BlockSpec((1, N_KV_HEADS, vrows, tm), lambda b, i: (b, 0, 0, i))],
        out_shape=[
            jax.ShapeDtypeStruct((B, L, POOL_WIDTH), BF16),
            jax.ShapeDtypeStruct((B, L, ATTN_WIDTH), F32),
            jax.ShapeDtypeStruct((B, ATTN_WIDTH, L), BF16),
            jax.ShapeDtypeStruct((B, L, KV_WIDTH), BF16),
            jax.ShapeDtypeStruct((B, N_KV_HEADS, vrows, L), BF16),
        ],
        scratch_shapes=[pltpu.VMEM((tm + 2 * HALO, D_MODEL), BF16),
                        pltpu.VMEM((tm + 2 * HALO, POOL_WIDTH), F32)],
        compiler_params=pltpu.CompilerParams(
            dimension_semantics=("arbitrary", "arbitrary"), vmem_limit_bytes=VMEM_LIMIT),
        name="even_in",
    )(x, x, x, norm_g, w_in, q_g, k_g, cos_tt, sin_tt, pool_w, pool_scale)


def _attn_kernel(qt_ref, k_ref, vx_ref, az_ref, o_ref, s_ref, m_ref):
    i = pl.program_id(0)
    n_tiles = pl.num_programs(0) - 1
    L = k_ref.shape[1]
    nq = L // Q_TILE
    nc = L // KEY_TILE
    R = Q_PER_KV * SUB_TILE

    @pl.when(i == 0)
    def _():
        s_ref[...] = jnp.zeros(s_ref.shape, F32)
        m_ref[...] = jnp.zeros(m_ref.shape, F32)

    kvh = jnp.right_shift(jnp.minimum(i, n_tiles - 1), nq.bit_length() - 1) & 1
    for sub in range(Q_TILE // SUB_TILE):
        tok = slice(sub * SUB_TILE, (sub + 1) * SUB_TILE)
        qt = qt_ref[0, :, tok]
        qcat = jnp.concatenate(
            [qt[g * HEAD_DIM:(g + 1) * HEAD_DIM, :] for g in range(Q_PER_KV)], axis=1)
        zero = jnp.zeros_like(qcat)
        qx = jnp.concatenate([jnp.where(kvh == 0, qcat, zero),
                              jnp.where(kvh == 1, qcat, zero)], axis=0)
        m_prev = jnp.max(m_ref[sub], axis=0, keepdims=True)
        m_new = None
        acc = None
        for c in range(nc):
            keys = slice(c * KEY_TILE, (c + 1) * KEY_TILE)
            s_c = jnp.dot(k_ref[0, keys, :], qx, preferred_element_type=F32)
            p = jnp.exp2(s_ref[sub, c] - m_prev).astype(BF16)
            d = jnp.dot(vx_ref[0, 0, :, keys], p, preferred_element_type=F32)
            acc = d if acc is None else acc + d
            s_ref[sub, c] = s_c
            mc = jnp.max(s_c.reshape(KEY_TILE // HALO, HALO, R), axis=0)
            m_new = mc if m_new is None else jnp.maximum(m_new, mc)
        m_ref[sub] = m_new
        out = acc[0:HEAD_DIM] * (1.0 / acc[HEAD_DIM:HEAD_DIM + 1])
        gate = _silu(az_ref[0, tok, :])
        for pair in range(Q_PER_KV // 2):
            two = jnp.concatenate(
                [out[:, (2 * pair) * SUB_TILE:(2 * pair + 1) * SUB_TILE],
                 out[:, (2 * pair + 1) * SUB_TILE:(2 * pair + 2) * SUB_TILE]], axis=0)
            lanes = slice(pair * LANES, (pair + 1) * LANES)
            o_ref[0, tok, lanes] = (two.T * gate[:, lanes]).astype(BF16)


def _attention(qt, k, vx, az):
    B, _, L = qt.shape
    tq = Q_TILE
    nq = L // tq
    gw = Q_PER_KV * HEAD_DIM
    n_tiles = B * N_KV_HEADS * nq
    nq_bits = nq.bit_length() - 1
    assert nq == 1 << nq_bits and N_KV_HEADS == 2

    def tile(t):
        return jnp.right_shift(t, nq_bits + 1), jnp.right_shift(t, nq_bits) & 1, t & (nq - 1)

    def cur(i):
        return tile(jnp.minimum(i, n_tiles - 1))

    def prev(i):
        return tile(jnp.maximum(i - 1, 0))

    def q_map(i):
        b, h, iq = cur(i)
        return (b, h, iq)

    def k_map(i):
        return (cur(i)[0], 0, 0)

    def v_map(i):
        b, h, _ = prev(i)
        return (b, h, 0, 0)

    def o_map(i):
        b, h, iq = prev(i)
        return (b, iq, h)

    R = Q_PER_KV * SUB_TILE
    nsub = tq // SUB_TILE
    return pl.pallas_call(
        _attn_kernel,
        grid=(n_tiles + 1,),
        in_specs=[
            pl.BlockSpec((1, gw, tq), q_map),
            pl.BlockSpec((1, L, KV_WIDTH), k_map),
            pl.BlockSpec((1, 1, HEAD_DIM + ONES_ROWS, L), v_map),
            pl.BlockSpec((1, tq, gw), o_map),
        ],
        out_specs=pl.BlockSpec((1, tq, gw), o_map),
        out_shape=jax.ShapeDtypeStruct((B, L, ATTN_WIDTH), BF16),
        scratch_shapes=[pltpu.VMEM((nsub, L // KEY_TILE, KEY_TILE, R), F32),
                        pltpu.VMEM((nsub, HALO, R), F32)],
        compiler_params=pltpu.CompilerParams(
            dimension_semantics=("arbitrary",), vmem_limit_bytes=VMEM_LIMIT),
        name="attention",
    )(qt, k, vx, az)


def _out_odd_kernel(a_ref, b_ref, x_ref, woe_ref, ng_ref, wi_ref, sg_ref, ws_ref, bs_ref, wo_ref,
                    out_ref, sv_ref):
    tm = x_ref.shape[1]
    parts = [slice(p * ROW_SUB, (p + 1) * ROW_SUB) for p in range(tm // ROW_SUB)]
    nch = ROW_SUB // CHUNK

    def proj(h, j):
        return jnp.dot(h, wi_ref[:, j * SGU_WIDTH:(j + 1) * SGU_WIDTH], preferred_element_type=F32)

    xs = [x_ref[0, rows] + jnp.dot(jnp.concatenate([a_ref[0, rows], b_ref[0, rows]], axis=1),
                                   woe_ref[...], preferred_element_type=F32) for rows in parts]
    hs = [_rms_rows(x, ng_ref[...]).astype(BF16) for x in xs]
    vvs = [_rms_rows(_gelu(proj(h, 1)), sg_ref[...]).astype(BF16) for h in hs]
    gus = [_gelu(proj(h, 0)) for h in hs]
    for p, vv in enumerate(vvs):
        for g in range(SGU_GROUPS):
            gs = slice(g * SGU_GC, (g + 1) * SGU_GC)
            vcat = jnp.concatenate([vv[n * CHUNK:(n + 1) * CHUNK, gs] for n in range(nch)], axis=1)
            res = jnp.dot(ws_ref[g], vcat, preferred_element_type=F32)
            for n in range(nch):
                r0 = p * ROW_SUB + n * CHUNK
                sv_ref[r0:r0 + CHUNK, gs] = res[:, n * SGU_GC:(n + 1) * SGU_GC] + bs_ref[g]
    ys = [(gus[p] * sv_ref[rows] * _silu(proj(hs[p], 2))).astype(BF16)
          for p, rows in enumerate(parts)]
    for p, rows in enumerate(parts):
        out_ref[0, rows] = xs[p] + jnp.dot(ys[p], wo_ref[...], preferred_element_type=F32)


def _out_odd(a, b, x, w_out_e, norm_g, w_in, sgu_g, w_s, b_s, w_out):
    B, L, _ = x.shape
    tm = ROW_TILE
    full = lambda shape: pl.BlockSpec(shape, lambda bb, i: (0,) * len(shape))
    tok = lambda w: pl.BlockSpec((1, tm, w), lambda bb, i: (bb, i, 0))
    return pl.pallas_call(
        _out_odd_kernel,
        grid=(B, L // tm),
        in_specs=[
            tok(POOL_WIDTH), tok(ATTN_WIDTH), tok(D_MODEL),
            full((D_MODEL, D_MODEL)),
            full((1, D_MODEL)),
            full((D_MODEL, 3 * SGU_WIDTH)),
            full((1, SGU_WIDTH)),
            full((SGU_GROUPS, CHUNK, CHUNK)),
            full((SGU_GROUPS, CHUNK, SGU_GC)),
            full((SGU_WIDTH, D_MODEL)),
        ],
        out_specs=tok(D_MODEL),
        out_shape=jax.ShapeDtypeStruct((B, L, D_MODEL), F32),
        scratch_shapes=[pltpu.VMEM((tm, SGU_WIDTH), F32)],
        compiler_params=pltpu.CompilerParams(
            dimension_semantics=("arbitrary", "arbitrary"), vmem_limit_bytes=VMEM_LIMIT),
        name="out_odd",
    )(a, b, x, w_out_e, norm_g, w_in, sgu_g, w_s, b_s, w_out)


def _rope_tables(L):
    rows_n = L // GRID_W
    row = jnp.repeat(jnp.arange(rows_n), GRID_W).astype(F32)
    col = jnp.tile(jnp.arange(GRID_W), rows_n).astype(F32)
    inv = 1.0 / (ROPE_THETA ** (jnp.arange(ROPE_FREQ, dtype=F32) / ROPE_FREQ))
    ang_r = row[:, None] * inv[None, :]
    ang_c = col[:, None] * inv[None, :]
    cos_tt = jnp.concatenate([jnp.cos(ang_r), jnp.cos(ang_c)], axis=-1).T
    sin_tt = jnp.concatenate([jnp.sin(ang_r), jnp.sin(ang_c)], axis=-1).T
    return cos_tt, sin_tt


def _trunk(x, params, tables):
    (norm_e, w_in_e, pool_w, pool_scale, q_norm, k_norm, w_out_e,
     norm_o, w_in_o, sgu_norm, w_s, b_s, w_out_o) = params
    for j in range(DEPTH // 2):
        q_g = jnp.broadcast_to(jnp.tile(q_norm[j], N_Q_HEADS)[:, None], (ATTN_WIDTH, IN_ROW_TILE))
        k_g = jnp.broadcast_to(jnp.tile(k_norm[j], N_KV_HEADS)[:, None], (KV_WIDTH, IN_ROW_TILE))
        a, az, qt, k, vx = _even_in(x, norm_e[j][None, :], w_in_e[j], q_g, k_g, tables[0], tables[1],
                                    pool_w[j], pool_scale[j][None, :])
        b = _attention(qt, k, vx, az)
        bs_b = jnp.broadcast_to(b_s[j][:, :, None], (SGU_GROUPS, CHUNK, SGU_GC))
        x = _out_odd(a, b, x, w_out_e[j], norm_o[j][None, :], w_in_o[j], sgu_norm[j][None, :],
                     w_s[j], bs_b, w_out_o[j])
    return x


def kernel(x_prompt, x_sample, norm_e, w_in_e, pool_w, pool_scale, q_norm, k_norm, w_out_e,
           norm_o, w_in_o, sgu_norm, w_s, b_s, w_out_o):
    params = (norm_e, w_in_e.astype(BF16), pool_w.astype(BF16), pool_scale, q_norm, k_norm,
              w_out_e.astype(BF16), norm_o, w_in_o.astype(BF16), sgu_norm, w_s.astype(BF16), b_s,
              w_out_o.astype(BF16))
    tables = _rope_tables(x_prompt.shape[1])
    y_prompt = _trunk(x_prompt, params, tables)
    y_sample = _trunk(x_sample, params, tables)
    return (y_prompt, y_sample)
```

```python
import math

import jax
import jax.numpy as jnp
import numpy as np
from jax import lax
from jax.experimental import pallas as pl
from jax.experimental.pallas import tpu as pltpu

F32 = jnp.float32
BF16 = jnp.bfloat16

D_MODEL = 1024
DEPTH = 4
GRID_W = 64
EPS = 1e-6

POOL_WIDTH = 512
POOL_WINDOWS = (2, 4, 8, 16)
POOL_GC = 128

HEAD_DIM = 64
N_Q_HEADS = 8
N_KV_HEADS = 2
Q_PER_KV = 4
ATTN_WIDTH = 512
KV_WIDTH = 128
ROPE_FREQ = 16
ROPE_THETA = 10000.0

OFF_PU, OFF_PZ, OFF_Q, OFF_K, OFF_V, OFF_AZ, EVEN_IN = 0, 512, 1024, 1536, 1664, 1792, 2304

SGU_WIDTH = 1024
SGU_GROUPS = 8
SGU_GC = 128
CHUNK = 128

LANES = 128
HALO = 8

ROW_TILE = 1024
IN_ROW_TILE = 1024
ROW_SUB = 256
Q_TILE = 512
SUB_TILE = 128
KEY_TILE = 256
ONES_ROWS = 16
Q_PRESCALE = HEAD_DIM ** -0.5 * math.log2(math.e)

VMEM_LIMIT = 56 * 1024 * 1024


def _silu(x):
    return x * (1.0 / (1.0 + jnp.exp(-x)))


def _gelu(x):
    return 0.5 * x * (1.0 + lax.erf(x * np.float32(math.sqrt(0.5))))


def _rms_rows(x, g):
    ms = jnp.mean(x * x, axis=-1, keepdims=True)
    return x * lax.rsqrt(ms + EPS) * g


def _norm_rope_t(t, gain, scale, cos_r, sin_r, cos_c, sin_c):
    ms = jnp.sum(t * t, axis=0, keepdims=True) * np.float32(1.0 / HEAD_DIM)
    y = t * (lax.rsqrt(ms + EPS) * np.float32(scale)) * gain
    x1, x2, x3, x4 = (y[j * ROPE_FREQ:(j + 1) * ROPE_FREQ] for j in range(4))
    return jnp.concatenate([x1 * cos_r - x2 * sin_r, x2 * cos_r + x1 * sin_r,
                            x3 * cos_c - x4 * sin_c, x4 * cos_c + x3 * sin_c], axis=0)


def _even_in_kernel(x_ref, xp_ref, xn_ref, ng_ref, w_ref, qg_ref, kg_ref, cost_ref, sint_ref,
                    pw_ref, ps_ref, a_ref, az_ref, qt_ref, k_ref, vx_ref, h_ref, ext_ref):
    it = pl.program_id(1)
    nt = pl.num_programs(1)
    tm = x_ref.shape[1]
    L = tm * nt
    ng = ng_ref[...]
    h_ref[0:tm] = _rms_rows(x_ref[0], ng).astype(BF16)
    h_ref[tm:tm + 2 * HALO] = _rms_rows(
        jnp.concatenate([xp_ref[0], xn_ref[0]], axis=0), ng).astype(BF16)

    def proj(lo, hi):
        return jnp.dot(h_ref[0:tm], w_ref[:, lo:hi], preferred_element_type=F32)

    u_all = jnp.dot(h_ref[...], w_ref[:, OFF_PU:OFF_PZ], preferred_element_type=F32)
    u = u_all[0:tm]
    ext_ref[0:HALO] = jnp.where(it > 0, u_all[tm:tm + HALO], 0.0)
    ext_ref[HALO:HALO + tm] = u
    ext_ref[HALO + tm:2 * HALO + tm] = jnp.where(it < nt - 1, u_all[tm + HALO:tm + 2 * HALO], 0.0)
    t = it * tm + lax.broadcasted_iota(jnp.int32, (tm, 1), 0)
    parts = []
    for g, w in enumerate(POOL_WINDOWS):
        hw = w // 2
        sl = slice(g * POOL_GC, (g + 1) * POOL_GC)
        acc = None
        for j in range(-hw, hw):
            piece = ext_ref[HALO + j:HALO + j + tm, sl]
            acc = piece if acc is None else acc + piece
        cnt = (jnp.minimum(t + hw, L) - jnp.maximum(t - hw, 0)).astype(F32)
        parts.append((acc / cnt - u[:, sl]).astype(BF16))

    cos_r, cos_c = cost_ref[0:ROPE_FREQ], cost_ref[ROPE_FREQ:2 * ROPE_FREQ]
    sin_r, sin_c = sint_ref[0:ROPE_FREQ], sint_ref[ROPE_FREQ:2 * ROPE_FREQ]
    qt = proj(OFF_Q, OFF_K).T
    for hd in range(N_Q_HEADS):
        rows = slice(hd * HEAD_DIM, (hd + 1) * HEAD_DIM)
        qt_ref[0, rows, :] = _norm_rope_t(qt[rows], qg_ref[rows], Q_PRESCALE,
                                          cos_r, sin_r, cos_c, sin_c).astype(BF16)
    kt = proj(OFF_K, OFF_V).T
    kt = jnp.concatenate(
        [_norm_rope_t(kt[kv * HEAD_DIM:(kv + 1) * HEAD_DIM], kg_ref[kv * HEAD_DIM:(kv + 1) * HEAD_DIM],
                      1.0, cos_r, sin_r, cos_c, sin_c) for kv in range(N_KV_HEADS)], axis=0)
    k_ref[0] = kt.T.astype(BF16)

    vt = proj(OFF_V, OFF_AZ).T
    ones = jnp.ones((ONES_ROWS, tm), BF16)
    for kv in range(N_KV_HEADS):
        vx_ref[0, kv, 0:HEAD_DIM] = vt[kv * HEAD_DIM:(kv + 1) * HEAD_DIM].astype(BF16)
        vx_ref[0, kv, HEAD_DIM:HEAD_DIM + ONES_ROWS] = ones

    gate = _silu(proj(OFF_PZ, OFF_Q))
    mixed = jnp.concatenate([jnp.dot(parts[g], pw_ref[g], preferred_element_type=F32)
                             for g in range(len(POOL_WINDOWS))], axis=1)
    a_ref[0] = (mixed * ps_ref[...] * gate).astype(BF16)
    az_ref[0] = proj(OFF_AZ, EVEN_IN)


def _even_in(x, norm_g, w_in, q_g, k_g, cos_tt, sin_tt, pool_w, pool_scale):
    B, L, _ = x.shape
    tm = IN_ROW_TILE
    nt = L // tm
    hb = tm // HALO
    full = lambda shape: pl.BlockSpec(shape, lambda b, i: (0,) * len(shape))
    tok = lambda w: pl.BlockSpec((1, tm, w), lambda b, i: (b, i, 0))
    vrows = HEAD_DIM + ONES_ROWS
    return pl.pallas_call(
        _even_in_kernel,
        grid=(B, nt),
        in_specs=[
            tok(D_MODEL),
            pl.BlockSpec((1, HALO, D_MODEL), lambda b, i: (b, jnp.maximum(i * hb - 1, 0), 0)),
            pl.BlockSpec((1, HALO, D_MODEL),
                         lambda b, i: (b, jnp.minimum((i + 1) * hb, L // HALO - 1), 0)),
            full((1, D_MODEL)),
            full((D_MODEL, EVEN_IN)),
            full((ATTN_WIDTH, tm)),
            full((KV_WIDTH, tm)),
            pl.BlockSpec((2 * ROPE_FREQ, tm), lambda b, i: (0, i)),
            pl.BlockSpec((2 * ROPE_FREQ, tm), lambda b, i: (0, i)),
            full((len(POOL_WINDOWS), POOL_GC, POOL_GC)),
            full((1, POOL_WIDTH)),
        ],
        out_specs=[tok(POOL_WIDTH), tok(ATTN_WIDTH),
                   pl.BlockSpec((1, ATTN_WIDTH, tm), lambda b, i: (b, 0, i)),
                   tok(KV_WIDTH),
                   pl.BlockSpec((1, N_KV_HEADS, vrows, tm), lambda b, i: (b, 0, 0, i))],
        out_shape=[
            jax.ShapeDtypeStruct((B, L, POOL_WIDTH), BF16),
            jax.ShapeDtypeStruct((B, L, ATTN_WIDTH), F32),
            jax.ShapeDtypeStruct((B, ATTN_WIDTH, L), BF16),
            jax.ShapeDtypeStruct((B, L, KV_WIDTH), BF16),
            jax.ShapeDtypeStruct((B, N_KV_HEADS, vrows, L), BF16),
        ],
        scratch_shapes=[pltpu.VMEM((tm + 2 * HALO, D_MODEL), BF16),
                        pltpu.VMEM((tm + 2 * HALO, POOL_WIDTH), F32)],
        compiler_params=pltpu.CompilerParams(
            dimension_semantics=("arbitrary", "arbitrary"), vmem_limit_bytes=VMEM_LIMIT),
        name="even_in",
    )(x, x, x, norm_g, w_in, q_g, k_g, cos_tt, sin_tt, pool_w, pool_scale)


def _attn_kernel(qt_ref, k_ref, vx_ref, az_ref, o_ref, s_ref, m_ref):
    i = pl.program_id(0)
    n_tiles = pl.num_programs(0) - 1
    L = k_ref.shape[1]
    nq = L // Q_TILE
    nc = L // KEY_TILE
    R = Q_PER_KV * SUB_TILE

    @pl.when(i == 0)
    def _():
        s_ref[...] = jnp.zeros(s_ref.shape, F32)
        m_ref[...] = jnp.zeros(m_ref.shape, F32)

    kvh = jnp.right_shift(jnp.minimum(i, n_tiles - 1), nq.bit_length() - 1) & 1
    for sub in range(Q_TILE // SUB_TILE):
        tok = slice(sub * SUB_TILE, (sub + 1) * SUB_TILE)
        qt = qt_ref[0, :, tok]
        qcat = jnp.concatenate(
            [qt[g * HEAD_DIM:(g + 1) * HEAD_DIM, :] for g in range(Q_PER_KV)], axis=1)
        zero = jnp.zeros_like(qcat)
        qx = jnp.concatenate([jnp.where(kvh == 0, qcat, zero),
                              jnp.where(kvh == 1, qcat, zero)], axis=0)
        m_prev = jnp.max(m_ref[sub], axis=0, keepdims=True)
        m_new = None
        acc = None
        for c in range(nc):
            keys = slice(c * KEY_TILE, (c + 1) * KEY_TILE)
            s_c = jnp.dot(k_ref[0, keys, :], qx, preferred_element_type=F32)
            p = jnp.exp2(s_ref[sub, c] - m_prev).astype(BF16)
            d = jnp.dot(vx_ref[0, 0, :, keys], p, preferred_element_type=F32)
            acc = d if acc is None else acc + d
            s_ref[sub, c] = s_c
            mc = jnp.max(s_c.reshape(KEY_TILE // HALO, HALO, R), axis=0)
            m_new = mc if m_new is None else jnp.maximum(m_new, mc)
        m_ref[sub] = m_new
        out = acc[0:HEAD_DIM] * (1.0 / acc[HEAD_DIM:HEAD_DIM + 1])
        gate = _silu(az_ref[0, tok, :])
        for pair in range(Q_PER_KV // 2):
            two = jnp.concatenate(
                [out[:, (2 * pair) * SUB_TILE:(2 * pair + 1) * SUB_TILE],
                 out[:, (2 * pair + 1) * SUB_TILE:(2 * pair + 2) * SUB_TILE]], axis=0)
            lanes = slice(pair * LANES, (pair + 1) * LANES)
            o_ref[0, tok, lanes] = (two.T * gate[:, lanes]).astype(BF16)


def _attention(qt, k, vx, az):
    B, _, L = qt.shape
    tq = Q_TILE
    nq = L // tq
    gw = Q_PER_KV * HEAD_DIM
    n_tiles = B * N_KV_HEADS * nq
    nq_bits = nq.bit_length() - 1
    assert nq == 1 << nq_bits and N_KV_HEADS == 2

    def tile(t):
        return jnp.right_shift(t, nq_bits + 1), jnp.right_shift(t, nq_bits) & 1, t & (nq - 1)

    def cur(i):
        return tile(jnp.minimum(i, n_tiles - 1))

    def prev(i):
        return tile(jnp.maximum(i - 1, 0))

    def q_map(i):
        b, h, iq = cur(i)
        return (b, h, iq)

    def k_map(i):
        return (cur(i)[0], 0, 0)

    def v_map(i):
        b, h, _ = prev(i)
        return (b, h, 0, 0)

    def o_map(i):
        b, h, iq = prev(i)
        return (b, iq, h)

    R = Q_PER_KV * SUB_TILE
    nsub = tq // SUB_TILE
    return pl.pallas_call(
        _attn_kernel,
        grid=(n_tiles + 1,),
        in_specs=[
            pl.BlockSpec((1, gw, tq), q_map),
            pl.BlockSpec((1, L, KV_WIDTH), k_map),
            pl.BlockSpec((1, 1, HEAD_DIM + ONES_ROWS, L), v_map),
            pl.BlockSpec((1, tq, gw), o_map),
        ],
        out_specs=pl.BlockSpec((1, tq, gw), o_map),
        out_shape=jax.ShapeDtypeStruct((B, L, ATTN_WIDTH), BF16),
        scratch_shapes=[pltpu.VMEM((nsub, L // KEY_TILE, KEY_TILE, R), F32),
                        pltpu.VMEM((nsub, HALO, R), F32)],
        compiler_params=pltpu.CompilerParams(
            dimension_semantics=("arbitrary",), vmem_limit_bytes=VMEM_LIMIT),
        name="attention",
    )(qt, k, vx, az)


def _out_odd_kernel(a_ref, b_ref, x_ref, woe_ref, ng_ref, wi_ref, sg_ref, ws_ref, bs_ref, wo_ref,
                    out_ref, sv_ref):
    tm = x_ref.shape[1]
    parts = [slice(p * ROW_SUB, (p + 1) * ROW_SUB) for p in range(tm // ROW_SUB)]
    nch = ROW_SUB // CHUNK

    def proj(h, j):
        return jnp.dot(h, wi_ref[:, j * SGU_WIDTH:(j + 1) * SGU_WIDTH], preferred_element_type=F32)

    xs = [x_ref[0, rows] + jnp.dot(jnp.concatenate([a_ref[0, rows], b_ref[0, rows]], axis=1),
                                   woe_ref[...], preferred_element_type=F32) for rows in parts]
    hs = [_rms_rows(x, ng_ref[...]).astype(BF16) for x in xs]
    vvs = [_rms_rows(_gelu(proj(h, 1)), sg_ref[...]).astype(BF16) for h in hs]
    gus = [_gelu(proj(h, 0)) for h in hs]
    for p, vv in enumerate(vvs):
        for g in range(SGU_GROUPS):
            gs = slice(g * SGU_GC, (g + 1) * SGU_GC)
            vcat = jnp.concatenate([vv[n * CHUNK:(n + 1) * CHUNK, gs] for n in range(nch)], axis=1)
            res = jnp.dot(ws_ref[g], vcat, preferred_element_type=F32)
            for n in range(nch):
                r0 = p * ROW_SUB + n * CHUNK
                sv_ref[r0:r0 + CHUNK, gs] = res[:, n * SGU_GC:(n + 1) * SGU_GC] + bs_ref[g]
    ys = [(gus[p] * sv_ref[rows] * _silu(proj(hs[p], 2))).astype(BF16)
          for p, rows in enumerate(parts)]
    for p, rows in enumerate(parts):
        out_ref[0, rows] = xs[p] + jnp.dot(ys[p], wo_ref[...], preferred_element_type=F32)


def _out_odd(a, b, x, w_out_e, norm_g, w_in, sgu_g, w_s, b_s, w_out):
    B, L, _ = x.shape
    tm = ROW_TILE
    full = lambda shape: pl.BlockSpec(shape, lambda bb, i: (0,) * len(shape))
    tok = lambda w: pl.BlockSpec((1, tm, w), lambda bb, i: (bb, i, 0))
    return pl.pallas_call(
        _out_odd_kernel,
        grid=(B, L // tm),
        in_specs=[
            tok(POOL_WIDTH), tok(ATTN_WIDTH), tok(D_MODEL),
            full((D_MODEL, D_MODEL)),
            full((1, D_MODEL)),
            full((D_MODEL, 3 * SGU_WIDTH)),
            full((1, SGU_WIDTH)),
            full((SGU_GROUPS, CHUNK, CHUNK)),
            full((SGU_GROUPS, CHUNK, SGU_GC)),
            full((SGU_WIDTH, D_MODEL)),
        ],
        out_specs=tok(D_MODEL),
        out_shape=jax.ShapeDtypeStruct((B, L, D_MODEL), F32),
        scratch_shapes=[pltpu.VMEM((tm, SGU_WIDTH), F32)],
        compiler_params=pltpu.CompilerParams(
            dimension_semantics=("arbitrary", "arbitrary"), vmem_limit_bytes=VMEM_LIMIT),
        name="out_odd",
    )(a, b, x, w_out_e, norm_g, w_in, sgu_g, w_s, b_s, w_out)


def _rope_tables(L):
    rows_n = L // GRID_W
    row = jnp.repeat(jnp.arange(rows_n), GRID_W).astype(F32)
    col = jnp.tile(jnp.arange(GRID_W), rows_n).astype(F32)
    inv = 1.0 / (ROPE_THETA ** (jnp.arange(ROPE_FREQ, dtype=F32) / ROPE_FREQ))
    ang_r = row[:, None] * inv[None, :]
    ang_c = col[:, None] * inv[None, :]
    cos_tt = jnp.concatenate([jnp.cos(ang_r), jnp.cos(ang_c)], axis=-1).T
    sin_tt = jnp.concatenate([jnp.sin(ang_r), jnp.sin(ang_c)], axis=-1).T
    return cos_tt, sin_tt


def _trunk(x, params, tables):
    (norm_e, w_in_e, pool_w, pool_scale, q_norm, k_norm, w_out_e,
     norm_o, w_in_o, sgu_norm, w_s, b_s, w_out_o) = params
    for j in range(DEPTH // 2):
        q_g = jnp.broadcast_to(jnp.tile(q_norm[j], N_Q_HEADS)[:, None], (ATTN_WIDTH, IN_ROW_TILE))
        k_g = jnp.broadcast_to(jnp.tile(k_norm[j], N_KV_HEADS)[:, None], (KV_WIDTH, IN_ROW_TILE))
        a, az, qt, k, vx = _even_in(x, norm_e[j][None, :], w_in_e[j], q_g, k_g, tables[0], tables[1],
                                    pool_w[j], pool_scale[j][None, :])
        b = _attention(qt, k, vx, az)
        bs_b = jnp.broadcast_to(b_s[j][:, :, None], (SGU_GROUPS, CHUNK, SGU_GC))
        x = _out_odd(a, b, x, w_out_e[j], norm_o[j][None, :], w_in_o[j], sgu_norm[j][None, :],
                     w_s[j], bs_b, w_out_o[j])
    return x


def kernel(x_prompt, x_sample, norm_e, w_in_e, pool_w, pool_scale, q_norm, k_norm, w_out_e,
           norm_o, w_in_o, sgu_norm, w_s, b_s, w_out_o):
    params = (norm_e, w_in_e.astype(BF16), pool_w.astype(BF16), pool_scale, q_norm, k_norm,
              w_out_e.astype(BF16), norm_o, w_in_o.astype(BF16), sgu_norm, w_s.astype(BF16), b_s,
              w_out_o.astype(BF16))
    tables = _rope_tables(x_prompt.shape[1])
    y_prompt = _trunk(x_prompt, params, tables)
    y_sample = _trunk(x_sample, params, tables)
    return (y_prompt, y_sample)
```
